```python
import jax, jax.numpy as jnp
from jax import lax
import numpy as np

D_MODEL = 4096
BATCH = 2
SEQ = 8192
DEPTH = 4

ATT_HEADS = 8
ATT_HEAD_DIM = 128
ATT_W = ATT_HEADS * ATT_HEAD_DIM
Q_BLOCK = 128
CONV_W = 1024
CONV_K = 3
RWKV_HEADS = 16
RWKV_HEAD_DIM = 64
RWKV_W = RWKV_HEADS * RWKV_HEAD_DIM
DECAY_LORA = 64
AAA_LORA = 64
GATE_LORA = 160
RWKV_SHIFT_W = 3 * RWKV_W + DECAY_LORA + AAA_LORA + GATE_LORA
N_BRANCH = 3
MERGE_RANK = 256
D_FF = 4 * D_MODEL
RMS_EPS = 1e-6
LNX_EPS = 64e-5
KK_EPS = 1e-12

IN_SPLIT_SIZES = (ATT_W, ATT_W, ATT_W, CONV_W, CONV_W, CONV_W, RWKV_SHIFT_W, MERGE_RANK)
IN_COLS = sum(IN_SPLIT_SIZES)
RWKV_SPLIT_SIZES = (RWKV_W, RWKV_W, RWKV_W, DECAY_LORA, AAA_LORA, GATE_LORA)

kernel_name = 'hybrid_sb_conv_rwkv7_trunk'


def rmsnorm(x, g):
    xf = x.astype(jnp.float32)
    y = xf * lax.rsqrt(jnp.mean(xf * xf, axis=-1, keepdims=True) + RMS_EPS)
    return (y * g.astype(jnp.float32)).astype(x.dtype)


def split_cols(x, sizes):
    offsets = np.cumsum(np.array(sizes))[:-1].tolist()
    return jnp.split(x, offsets, axis=-1)


def stick_breaking_attention(q, k, v):
    b, s, h, d = q.shape
    nb = s // Q_BLOCK
    scale = d ** -0.5
    qb = q.reshape(b, nb, Q_BLOCK, h, d).transpose(1, 0, 3, 2, 4)
    kh = k.transpose(0, 2, 1, 3)
    vh = v.transpose(0, 2, 1, 3)
    key_pos = jnp.arange(s, dtype=jnp.int32)

    def block(args):
        q_blk, start = args
        z = jnp.einsum('bhqd,bhkd->bhqk', q_blk, kh).astype(jnp.float32) * scale
        q_pos = start + jnp.arange(Q_BLOCK, dtype=jnp.int32)
        causal = key_pos[None, :] < q_pos[:, None]
        sp = jnp.where(causal, jax.nn.softplus(z), 0.0)
        after = lax.cumsum(sp, axis=3, reverse=True) - sp
        w = jnp.where(causal, jnp.exp(jax.nn.log_sigmoid(z) - after), 0.0)
        return jnp.einsum('bhqk,bhkd->bhqd', w.astype(v.dtype), vh)

    starts = jnp.arange(nb, dtype=jnp.int32) * Q_BLOCK
    o = lax.map(block, (qb, starts))
    return o.transpose(1, 0, 3, 2, 4).reshape(b, s, h * d)


def short_gated_conv(b_gate, c_gate, u, conv_w):
    y = lax.conv_general_dilated(c_gate * u, conv_w[:, None, :], window_strides=(1,),
                                 padding=((CONV_K - 1, 0),),
                                 dimension_numbers=('NWC', 'WIO', 'NWC'),
                                 feature_group_count=CONV_W)
    return b_gate * y


def token_shift(p, mu):
    prev = jnp.pad(p, ((0, 0), (1, 0), (0, 0)))[:, :-1]
    return p + mu * (prev - p)


def wkv7_scan(r, decay, k, v, a_vec, b_vec):
    f32 = jnp.float32
    xs = tuple(jnp.moveaxis(t.astype(f32), 1, 0) for t in (r, decay, k, v, a_vec, b_vec))
    bsz, _, h, n = r.shape

    def step(state, inp):
        r_t, w_t, k_t, v_t, a_t, b_t = inp
        sa = jnp.einsum('bhvk,bhk->bhv', state, a_t)
        state = (state * w_t[:, :, None, :] + sa[..., None] * b_t[:, :, None, :]
                 + v_t[..., None] * k_t[:, :, None, :])
        return state, jnp.einsum('bhvk,bhk->bhv', state, r_t)

    s0 = jnp.zeros((bsz, h, n, n), f32)
    _, y = lax.scan(step, s0, xs)
    return jnp.moveaxis(y, 0, 1)


def rwkv7_time_mix(seg, mu, w0, w_decay_up, a0, w_a_up, w_g_up, k_k, k_a, r_k, lnx_w, lnx_b):
    b, s, _ = seg.shape
    seg = token_shift(seg, mu)
    r, k, v, lw, la, lg = split_cols(seg, RWKV_SPLIT_SIZES)
    heads = lambda t: t.reshape(b, s, RWKV_HEADS, RWKV_HEAD_DIM)
    w = -jax.nn.softplus(-(w0 + jnp.tanh(lw) @ w_decay_up)) - 0.5
    decay = jnp.exp(-jnp.exp(w.astype(jnp.float32)))
    a = jax.nn.sigmoid(a0 + la @ w_a_up)
    g = jax.nn.sigmoid(lg) @ w_g_up
    kk = heads(k * k_k).astype(jnp.float32)
    kk = kk / jnp.maximum(jnp.sqrt(jnp.sum(kk * kk, axis=-1, keepdims=True)), KK_EPS)
    k = k * (1.0 + (a - 1.0) * k_a)
    y = wkv7_scan(heads(r), heads(decay), heads(k), heads(v), -kk, kk * heads(a))
    mean = jnp.mean(y, axis=-1, keepdims=True)
    var = jnp.mean(jnp.square(y - mean), axis=-1, keepdims=True)
    yn = ((y - mean) * lax.rsqrt(var + LNX_EPS)).reshape(b, s, RWKV_W) * lnx_w + lnx_b
    bonus = jnp.sum(heads(r) * heads(k) * r_k, axis=-1, keepdims=True) * heads(v)
    return ((yn + bonus.reshape(b, s, RWKV_W)) * g).astype(seg.dtype)


def setup_inputs(seed: int = 0) -> dict:
    key = jax.random.key(seed)
    ks = list(jax.random.split(key, 25))
    L = DEPTH
    f32 = jnp.float32
    nrm = lambda i, shape, scale: jax.random.normal(ks[i], shape, f32) * scale
    uni = lambda i, shape, lo, hi: jax.random.uniform(ks[i], shape, f32, lo, hi)
    return {
        'x': nrm(0, (BATCH, SEQ, D_MODEL), 1.0),
        'norm_mix': 1.0 + nrm(1, (L, D_MODEL), 0.02),
        'w_in': nrm(2, (L, D_MODEL, IN_COLS), D_MODEL ** -0.5),
        'w_att_o': nrm(3, (L, ATT_W, D_MODEL), ATT_W ** -0.5),
        'conv_w': nrm(4, (L, CONV_K, CONV_W), CONV_K ** -0.5),
        'w_conv_o': nrm(5, (L, CONV_W, D_MODEL), CONV_W ** -0.5),
        'rwkv_mu': uni(6, (L, RWKV_SHIFT_W), 0.0, 1.0),
        'rwkv_w0': uni(7, (L, RWKV_W), -6.0, -1.0),
        'rwkv_w_decay_up': nrm(8, (L, DECAY_LORA, RWKV_W), 0.5 * DECAY_LORA ** -0.5),
        'rwkv_a0': nrm(9, (L, RWKV_W), 0.5),
        'rwkv_w_a_up': nrm(10, (L, AAA_LORA, RWKV_W), 0.5 * AAA_LORA ** -0.5),
        'rwkv_w_g_up': nrm(11, (L, GATE_LORA, RWKV_W), GATE_LORA ** -0.5),
        'rwkv_k_k': 0.85 + nrm(12, (L, RWKV_W), 0.02),
        'rwkv_k_a': 1.0 + nrm(13, (L, RWKV_W), 0.02),
        'rwkv_r_k': nrm(14, (L, RWKV_HEADS, RWKV_HEAD_DIM), 0.1),
        'rwkv_lnx_w': 1.0 + nrm(15, (L, RWKV_W), 0.02),
        'rwkv_lnx_b': nrm(16, (L, RWKV_W), 0.02),
        'w_rwkv_o': nrm(17, (L, RWKV_W, D_MODEL), RWKV_W ** -0.5),
        'w_gate_up': nrm(18, (L, MERGE_RANK, N_BRANCH * D_MODEL), MERGE_RANK ** -0.5),
        'b_gate': nrm(19, (L, N_BRANCH * D_MODEL), 0.02),
        'w_out': nrm(20, (L, D_MODEL, D_MODEL), D_MODEL ** -0.5),
        'norm_mlp': 1.0 + nrm(21, (L, D_MODEL), 0.02),
        'w_mlp_up': nrm(22, (L, D_MODEL, D_FF), D_MODEL ** -0.5),
        'w_mlp_down': nrm(23, (L, D_FF, D_MODEL), D_FF ** -0.5),
        'norm_final': 1.0 + nrm(24, (D_MODEL,), 0.02),
    }


def reference(x, norm_mix, w_in, w_att_o, conv_w, w_conv_o, rwkv_mu, rwkv_w0, rwkv_w_decay_up,
              rwkv_a0, rwkv_w_a_up, rwkv_w_g_up, rwkv_k_k, rwkv_k_a, rwkv_r_k, rwkv_lnx_w,
              rwkv_lnx_b, w_rwkv_o, w_gate_up, b_gate, w_out, norm_mlp, w_mlp_up, w_mlp_down,
              norm_final):
    b, s, _ = x.shape
    att_heads = lambda t: t.reshape(b, s, ATT_HEADS, ATT_HEAD_DIM)
    for l in range(DEPTH):
        h = rmsnorm(x, norm_mix[l])
        p = h @ w_in[l]
        q, k, v, c_b, c_c, c_u, rw, gate_down = split_cols(p, IN_SPLIT_SIZES)
        y_att = stick_breaking_attention(att_heads(q), att_heads(k), att_heads(v)) @ w_att_o[l]
        y_conv = short_gated_conv(c_b, c_c, c_u, conv_w[l]) @ w_conv_o[l]
        y_rwkv = rwkv7_time_mix(rw, rwkv_mu[l], rwkv_w0[l], rwkv_w_decay_up[l], rwkv_a0[l],
                                rwkv_w_a_up[l], rwkv_w_g_up[l], rwkv_k_k[l], rwkv_k_a[l],
                                rwkv_r_k[l], rwkv_lnx_w[l], rwkv_lnx_b[l]) @ w_rwkv_o[l]
        gates = jax.nn.sigmoid(gate_down @ w_gate_up[l] + b_gate[l])
        g_att, g_conv, g_rwkv = jnp.split(gates, N_BRANCH, axis=-1)
        x = x + (g_att * y_att + g_conv * y_conv + g_rwkv * y_rwkv) @ w_out[l]
        h = rmsnorm(x, norm_mlp[l])
        x = x + jnp.square(jax.nn.relu(h @ w_mlp_up[l])) @ w_mlp_down[l]
    return rmsnorm(x, norm_final)
```

```python
import functools

import jax
import jax.numpy as jnp
from jax import lax
from jax.experimental import pallas as pl
from jax.experimental.pallas import tpu as pltpu

F32 = jnp.float32
BF16 = jnp.bfloat16
HIGHEST = lax.Precision.HIGHEST

LANES = 128
SUBLANES = 8
VMEM_LIMIT_BYTES = 56 * 1024 * 1024

ATT_HEADS = 8
ATT_HEAD_DIM = 128
ATT_W = ATT_HEADS * ATT_HEAD_DIM
CONV_W = 1024
CONV_K = 3
RWKV_HEADS = 16
RWKV_HEAD_DIM = 64
RWKV_W = RWKV_HEADS * RWKV_HEAD_DIM
DECAY_LORA = 64
AAA_LORA = 64
GATE_LORA = 160
GATE_LORA_PAD = 256
LORA_BLOCK = DECAY_LORA + AAA_LORA
RWKV_COLS = 3 * RWKV_W + LORA_BLOCK + GATE_LORA_PAD
MERGE_RANK = 256
N_BRANCH = 3
RMS_EPS = 1e-6
LNX_EPS = 64e-5
KK_EPS = 1e-12
RWKV_CHUNK = 64
ATT_BLOCK = 256


def _tile(dim, target, mult):
    best = None
    t = mult
    while t <= min(dim, target):
        if dim % t == 0:
            best = t
        t += mult
    return best if best is not None else dim


def _params(*sem):
    return pltpu.CompilerParams(dimension_semantics=sem, vmem_limit_bytes=VMEM_LIMIT_BYTES)


def _rmsnorm_kernel(x_ref, g_ref, o_ref):
    x = x_ref[...]
    ms = jnp.mean(x * x, axis=-1, keepdims=True)
    o_ref[...] = (x * lax.rsqrt(ms + RMS_EPS) * g_ref[...]).astype(o_ref.dtype)


def rmsnorm(x, g, out_dtype):
    t, d = x.shape
    tm = _tile(t, 256, SUBLANES)
    return pl.pallas_call(
        _rmsnorm_kernel,
        grid=(t // tm,),
        in_specs=[pl.BlockSpec((tm, d), lambda i: (i, 0)),
                  pl.BlockSpec((1, d), lambda i: (0, 0))],
        out_specs=pl.BlockSpec((tm, d), lambda i: (i, 0)),
        out_shape=jax.ShapeDtypeStruct((t, d), out_dtype),
        compiler_params=_params("parallel"),
        name="rmsnorm",
    )(x, g.reshape(1, d))


def _mm_kernel(*refs, epilogue, nk):
    if epilogue == "residual":
        a_ref, w_ref, res_ref, o_ref, acc_ref = refs
    else:
        a_ref, w_ref, o_ref, acc_ref = refs
        res_ref = None
    k = pl.program_id(2)

    @pl.when(k == 0)
    def _():
        acc_ref[...] = jnp.zeros_like(acc_ref)

    acc_ref[...] += jnp.dot(a_ref[...], w_ref[...], preferred_element_type=F32)

    @pl.when(k == nk - 1)
    def _():
        acc = acc_ref[...]
        if epilogue == "relu2":
            acc = jnp.square(jnp.maximum(acc, 0.0))
        elif epilogue == "residual":
            acc = acc + res_ref[...]
        o_ref[...] = acc.astype(o_ref.dtype)


def matmul(a, w, out_dtype, epilogue="none", res=None, tm=1024, tn=1024, tk=1024):
    m, kdim = a.shape
    n = w.shape[1]
    tm = _tile(m, tm, SUBLANES)
    tn = _tile(n, tn, LANES)
    tk = _tile(kdim, tk, LANES)
    nk = kdim // tk
    in_specs = [pl.BlockSpec((tm, tk), lambda i, j, k: (i, k)),
                pl.BlockSpec((tk, tn), lambda i, j, k: (k, j))]
    args = [a, w]
    if epilogue == "residual":
        in_specs.append(pl.BlockSpec((tm, tn), lambda i, j, k: (i, j)))
        args.append(res)
    return pl.pallas_call(
        functools.partial(_mm_kernel, epilogue=epilogue, nk=nk),
        grid=(m // tm, n // tn, nk),
        in_specs=in_specs,
        out_specs=pl.BlockSpec((tm, tn), lambda i, j, k: (i, j)),
        out_shape=jax.ShapeDtypeStruct((m, n), out_dtype),
        scratch_shapes=[pltpu.VMEM((tm, tn), F32)],
        compiler_params=_params("parallel", "parallel", "arbitrary"),
        name="matmul_" + epilogue,
    )(*args)


def _softplus(z):
    return jnp.maximum(z, 0.0) + jnp.log1p(jnp.exp(-jnp.abs(z)))


def _att_kernel(q_ref, k_ref, v_ref, o_ref, acc_ref, carry_ref, *, blk, scale):
    qi = pl.program_id(2)
    q = q_ref[...]
    row = lax.broadcasted_iota(jnp.int32, (blk, blk), 0)
    col = lax.broadcasted_iota(jnp.int32, (blk, blk), 1)
    later = (row > col).astype(BF16)
    causal = col < row

    def visit(kstart, masked):
        kb = k_ref[pl.ds(kstart, blk), :]
        vb = v_ref[pl.ds(kstart, blk), :]
        z = lax.dot_general(q, kb, (((1,), (1,)), ((), ())), preferred_element_type=F32) * scale
        sp_raw = _softplus(z)
        sp = jnp.where(causal, sp_raw, 0.0) if masked else sp_raw
        hi = sp.astype(BF16)
        lo = (sp - hi.astype(F32)).astype(BF16)
        after = (jnp.dot(hi, later, preferred_element_type=F32)
                 + jnp.dot(lo, later, preferred_element_type=F32))
        carry = carry_ref[...]
        w = jnp.exp(z - sp_raw - after - carry)
        if masked:
            w = jnp.where(causal, w, 0.0)
        acc_ref[...] += jnp.dot(w.astype(BF16), vb, preferred_element_type=F32)
        carry_ref[...] = carry + after[:, 0:1] + sp[:, 0:1]

    acc_ref[...] = jnp.zeros_like(acc_ref)
    carry_ref[...] = jnp.zeros_like(carry_ref)
    visit(pl.multiple_of(qi * blk, blk), True)

    def body(j, c):
        visit(pl.multiple_of((qi - 1 - j) * blk, blk), False)
        return c

    lax.fori_loop(0, qi, body, 0)
    o_ref[...] = acc_ref[...].astype(o_ref.dtype)


def stick_breaking_attention(p_att, batch, seq):
    blk = _tile(seq, ATT_BLOCK, LANES)
    nq = seq // blk
    d = ATT_HEAD_DIM
    return pl.pallas_call(
        functools.partial(_att_kernel, blk=blk, scale=d ** -0.5),
        grid=(batch, ATT_HEADS, nq),
        in_specs=[pl.BlockSpec((blk, d), lambda b, h, i: (b * nq + i, h)),
                  pl.BlockSpec((seq, d), lambda b, h, i: (b, ATT_HEADS + h)),
                  pl.BlockSpec((seq, d), lambda b, h, i: (b, 2 * ATT_HEADS + h))],
        out_specs=pl.BlockSpec((blk, d), lambda b, h, i: (b * nq + i, h)),
        out_shape=jax.ShapeDtypeStruct((batch * seq, ATT_W), BF16),
        scratch_shapes=[pltpu.VMEM((blk, d), F32), pltpu.VMEM((blk, 1), F32)],
        compiler_params=_params("parallel", "parallel", "arbitrary"),
        name="stick_breaking_attention",
    )(p_att, p_att, p_att)


def _shift_rows(x, halo, n):
    row = lax.broadcasted_iota(jnp.int32, x.shape, 0)
    out = pltpu.roll(x, n, 0)
    for r in range(n):
        out = jnp.where(row == r, halo[SUBLANES - n + r:SUBLANES - n + r + 1, :], out)
    return out


def _conv_kernel(p_ref, halo_ref, w_ref, o_ref, *, tiles_per_seq):
    i = pl.program_id(0)
    first = (i % tiles_per_seq) == 0
    cw = CONV_W
    x = p_ref[:, cw:2 * cw] * p_ref[:, 2 * cw:3 * cw]
    hx = halo_ref[:, cw:2 * cw] * halo_ref[:, 2 * cw:3 * cw]
    hx = jnp.where(first, 0.0, hx)
    x1 = _shift_rows(x, hx, 1)
    x2 = _shift_rows(x, hx, 2)
    y = w_ref[0:1, :] * x2 + w_ref[1:2, :] * x1 + w_ref[2:3, :] * x
    o_ref[...] = (p_ref[:, 0:cw] * y).astype(o_ref.dtype)


def short_gated_conv(p_conv, conv_w, seq):
    t = p_conv.shape[0]
    ts = _tile(seq, 512, SUBLANES)
    hb = ts // SUBLANES
    return pl.pallas_call(
        functools.partial(_conv_kernel, tiles_per_seq=seq // ts),
        grid=(t // ts,),
        in_specs=[pl.BlockSpec((ts, 3 * CONV_W), lambda i: (i, 0)),
                  pl.BlockSpec((SUBLANES, 3 * CONV_W), lambda i: (jnp.maximum(i * hb - 1, 0), 0)),
                  pl.BlockSpec((CONV_K, CONV_W), lambda i: (0, 0))],
        out_specs=pl.BlockSpec((ts, CONV_W), lambda i: (i, 0)),
        out_shape=jax.ShapeDtypeStruct((t, CONV_W), BF16),
        compiler_params=_params("parallel"),
        name="short_gated_conv",
    )(p_conv, p_conv, conv_w)


def _head_sum_matrix(n, value):
    r = lax.broadcasted_iota(jnp.int32, (n, n), 0) // RWKV_HEAD_DIM
    c = lax.broadcasted_iota(jnp.int32, (n, n), 1) // RWKV_HEAD_DIM
    return jnp.where(r == c, value, 0.0).astype(F32)


def _rw_prep_kernel(p_ref, halo_ref, mu_ref, w0_ref, wd_ref, a0_ref, wa_ref, wg_ref, kk_ref, ka_ref,
                    r_ref, lw_ref, k_ref, v_ref, a_ref, b_ref, g_ref, *, tiles_per_seq):
    i = pl.program_id(0)
    first = (i % tiles_per_seq) == 0
    p = p_ref[...]
    halo = jnp.where(first, 0.0, halo_ref[...])
    prev = _shift_rows(p, halo, 1)
    seg = p + mu_ref[...] * (prev - p)
    w = RWKV_W
    r = seg[:, 0:w]
    k = seg[:, w:2 * w]
    v = seg[:, 2 * w:3 * w]
    lora = seg[:, 3 * w:3 * w + LORA_BLOCK]
    lg = seg[:, 3 * w + LORA_BLOCK:]
    wlog = w0_ref[...] + jnp.dot(jnp.tanh(lora), wd_ref[...], precision=HIGHEST, preferred_element_type=F32)
    wlog = -_softplus(-wlog) - 0.5
    lw_ref[...] = -jnp.exp(wlog)
    ag = jax.nn.sigmoid(a0_ref[...] + jnp.dot(lora, wa_ref[...], precision=HIGHEST, preferred_element_type=F32))
    g_ref[...] = jnp.dot(jax.nn.sigmoid(lg), wg_ref[...], precision=HIGHEST, preferred_element_type=F32)
    kk = k * kk_ref[...]
    head_ones = _head_sum_matrix(LANES, 1.0)
    sq = kk * kk
    ss = jnp.concatenate(
        [jnp.dot(sq[:, c:c + LANES], head_ones, precision=HIGHEST, preferred_element_type=F32)
         for c in range(0, w, LANES)], axis=1)
    kk = kk / jnp.maximum(jnp.sqrt(ss), KK_EPS)
    r_ref[...] = r
    k_ref[...] = k * (1.0 + (ag - 1.0) * ka_ref[...])
    v_ref[...] = v
    a_ref[...] = -kk
    b_ref[...] = kk * ag


def rwkv_prep(p_rw, mu, w0, wd, a0, wa, wg, k_k, k_a, seq):
    t = p_rw.shape[0]
    ts = _tile(seq, 256, SUBLANES)
    hb = ts // SUBLANES
    w = RWKV_W
    row = lambda n: pl.BlockSpec((1, n), lambda i: (0, 0))
    full = lambda a: pl.BlockSpec(a.shape, lambda i: (0, 0))
    out = jax.ShapeDtypeStruct((t, w), F32)
    return pl.pallas_call(
        functools.partial(_rw_prep_kernel, tiles_per_seq=seq // ts),
        grid=(t // ts,),
        in_specs=[pl.BlockSpec((ts, RWKV_COLS), lambda i: (i, 0)),
                  pl.BlockSpec((SUBLANES, RWKV_COLS), lambda i: (jnp.maximum(i * hb - 1, 0), 0)),
                  row(RWKV_COLS), row(w), full(wd), row(w), full(wa), full(wg), row(w), row(w)],
        out_specs=[pl.BlockSpec((ts, w), lambda i: (i, 0))] * 7,
        out_shape=[out] * 7,
        compiler_params=_params("parallel"),
        name="rwkv_prep",
    )(p_rw, p_rw, mu.reshape(1, -1), w0.reshape(1, w), wd, a0.reshape(1, w), wa, wg,
      k_k.reshape(1, w), k_a.reshape(1, w))


def _dot_hi(x, y):
    return jnp.dot(x, y, precision=HIGHEST, preferred_element_type=F32)


def _dot_nt_hi(x, y):
    return lax.dot_general(x, y, (((1,), (1,)), ((), ())), precision=HIGHEST, preferred_element_type=F32)


def _rw_scan_kernel(r_ref, lw_ref, k_ref, v_ref, a_ref, b_ref, g_ref, rk_ref, lnw_ref, lnb_ref,
                    o_ref, z_ref, *, chunk):
    c = chunk
    n = 2 * RWKV_HEAD_DIM
    assert n == LANES and 2 * c == n

    @pl.when(pl.program_id(2) == 0)
    def _():
        z_ref[...] = jnp.zeros_like(z_ref)

    r = r_ref[...]
    lw = lw_ref[...]
    k = k_ref[...]
    v = v_ref[...]
    a = a_ref[...]
    b = b_ref[...]

    row = lax.broadcasted_iota(jnp.int32, (c, n), 0)
    lane = lax.broadcasted_iota(jnp.int32, (c, n), 1)
    lane_a = lane < RWKV_HEAD_DIM
    src = jnp.where(lane_a, lane, lane - RWKV_HEAD_DIM)
    strict = src < row
    incl = src <= row
    sq_r = lax.broadcasted_iota(jnp.int32, (n, n), 0)
    sq_c = lax.broadcasted_iota(jnp.int32, (n, n), 1)
    same_head = (sq_r < RWKV_HEAD_DIM) == (sq_c < RWKV_HEAD_DIM)
    eye = sq_r == sq_c

    ti = lax.broadcasted_iota(jnp.int32, (c, c), 0)
    si = lax.broadcasted_iota(jnp.int32, (c, c), 1)
    cum = _dot_hi((si <= ti).astype(F32), lw)
    cum_end = cum[c - 1:c, :]
    a_hat = a * jnp.exp(cum - lw)
    r_hat = r * jnp.exp(cum)
    inv = jnp.exp(-cum)
    b_chk = b * inv
    k_chk = k * inv
    to_end = jnp.exp(cum_end - cum)
    b_end = b * to_end
    k_end = k * to_end
    gamma_end = jnp.exp(cum_end)

    zero = jnp.zeros_like(a_hat)
    a_hat_a = jnp.where(lane_a, a_hat, zero)
    a_hat_b = jnp.where(lane_a, zero, a_hat)
    x_a = jnp.concatenate([a_hat_a, jnp.where(lane_a, r_hat, zero)], axis=0)
    x_b = jnp.concatenate([a_hat_b, jnp.where(lane_a, zero, r_hat)], axis=0)
    m_a = _dot_nt_hi(x_a, jnp.concatenate([b_chk, k_chk], axis=0))
    m_b = _dot_nt_hi(x_b, jnp.concatenate([k_chk, b_chk], axis=0))
    ma_top = jnp.where(strict, m_a[:c], 0.0)
    ma_bot = jnp.where(incl, m_a[c:], 0.0)
    mb_top = jnp.where(strict, m_b[:c], 0.0)
    mb_bot = jnp.where(incl, m_b[c:], 0.0)

    nil = jnp.concatenate([jnp.where(lane_a, ma_top, 0.0), jnp.where(lane_a, 0.0, mb_top)], axis=0)
    t_inv = jnp.where(eye, 1.0, 0.0) + nil
    power = nil
    steps = c.bit_length() - 2
    for _ in range(steps):
        power = _dot_hi(power, power)
        t_inv = t_inv + _dot_hi(t_inv, power)

    def fold(stacked):
        kept = jnp.where(same_head, stacked, 0.0)
        return kept[:c] + kept[c:]

    a_new = fold(_dot_hi(t_inv, jnp.concatenate([a_hat_a, a_hat_b], axis=0)))
    l_ak = jnp.concatenate([jnp.where(lane_a, 0.0, ma_top), jnp.where(lane_a, mb_top, 0.0)], axis=0)
    vv = jnp.concatenate([v, v], axis=0)
    w1 = jnp.where(same_head, _dot_hi(l_ak, vv), 0.0)
    v_new = fold(_dot_hi(t_inv, w1))
    l_rb = jnp.concatenate([jnp.where(lane_a, ma_bot, 0.0), jnp.where(lane_a, 0.0, mb_bot)], axis=0)
    r_new = r_hat + fold(_dot_hi(l_rb, jnp.concatenate([a_new, a_new], axis=0)))
    y_in = jnp.where(lane_a,
                     _dot_hi(ma_bot, jnp.concatenate([v_new, v], axis=0)),
                     _dot_hi(mb_bot, jnp.concatenate([v, v_new], axis=0)))
    g_mat = (jnp.where(same_head, _dot_hi(b_end.T, a_new), 0.0)
             + jnp.where(eye, jnp.broadcast_to(gamma_end, (n, n)), 0.0))
    h_mat = jnp.where(same_head,
                      _dot_hi(jnp.concatenate([b_end, k_end], axis=0).T,
                              jnp.concatenate([v_new, v], axis=0)), 0.0)

    z = z_ref[...]
    y = _dot_hi(r_new, z) + y_in
    z_ref[...] = _dot_hi(g_mat, z) + h_mat

    head_mean = jnp.where(same_head, 1.0 / RWKV_HEAD_DIM, 0.0).astype(F32)
    head_ones = jnp.where(same_head, 1.0, 0.0).astype(F32)
    mean = _dot_hi(y, head_mean)
    dev = y - mean
    var = _dot_hi(dev * dev, head_mean)
    yn = dev * lax.rsqrt(var + LNX_EPS) * lnw_ref[...] + lnb_ref[...]
    bonus = _dot_hi(r * k * rk_ref[...], head_ones) * v
    o_ref[...] = ((yn + bonus) * g_ref[...]).astype(o_ref.dtype)


def rwkv_scan(r, lw, k, v, a, b, g, r_k, lnx_w, lnx_b, batch, seq):
    t = r.shape[0]
    c = RWKV_CHUNK
    nc = seq // c
    n = 2 * RWKV_HEAD_DIM
    pairs = RWKV_W // n
    tok = pl.BlockSpec((c, n), lambda bi, p, ci: (bi * nc + ci, p))
    par = pl.BlockSpec((1, n), lambda bi, p, ci: (0, p))
    return pl.pallas_call(
        functools.partial(_rw_scan_kernel, chunk=c),
        grid=(batch, pairs, nc),
        in_specs=[tok] * 7 + [par] * 3,
        out_specs=tok,
        out_shape=jax.ShapeDtypeStruct((t, RWKV_W), BF16),
        scratch_shapes=[pltpu.VMEM((n, n), F32)],
        compiler_params=_params("parallel", "parallel", "arbitrary"),
        name="rwkv_scan",
    )(r, lw, k, v, a, b, g, r_k.reshape(1, RWKV_W), lnx_w.reshape(1, RWKV_W), lnx_b.reshape(1, RWKV_W))


def _merge_kernel(att_ref, conv_ref, rw_ref, gd_ref, wa_ref, wc_ref, wr_ref,
                  ga_ref, gc_ref, gr_ref, ba_ref, bc_ref, br_ref, o_ref):
    gd = gd_ref[...]

    def branch(x_ref, w_ref, gw_ref, gb_ref):
        y = jnp.dot(x_ref[...], w_ref[...], preferred_element_type=F32)
        gate = jax.nn.sigmoid(jnp.dot(gd, gw_ref[...], preferred_element_type=F32) + gb_ref[...])
        return gate * y

    out = branch(att_ref, wa_ref, ga_ref, ba_ref)
    out = out + branch(conv_ref, wc_ref, gc_ref, bc_ref)
    out = out + branch(rw_ref, wr_ref, gr_ref, br_ref)
    o_ref[...] = out.astype(o_ref.dtype)


def gated_merge(att, conv, rw, gd, w_att_o, w_conv_o, w_rwkv_o, w_gate_up, b_gate):
    t = att.shape[0]
    d = w_att_o.shape[1]
    tm = _tile(t, 1024, SUBLANES)
    tn = _tile(d, 512, LANES)
    nj = d // tn
    act = lambda w: pl.BlockSpec((tm, w), lambda i, j: (i, 0))
    wout = lambda w: pl.BlockSpec((w, tn), lambda i, j: (0, j))
    gate_w = lambda br: pl.BlockSpec((MERGE_RANK, tn), lambda i, j: (0, br * nj + j))
    gate_b = lambda br: pl.BlockSpec((1, tn), lambda i, j: (0, br * nj + j))
    bg = b_gate.reshape(1, N_BRANCH * d)
    return pl.pallas_call(
        _merge_kernel,
        grid=(t // tm, nj),
        in_specs=[act(ATT_W), act(CONV_W), act(RWKV_W), act(MERGE_RANK),
                  wout(ATT_W), wout(CONV_W), wout(RWKV_W),
                  gate_w(0), gate_w(1), gate_w(2), gate_b(0), gate_b(1), gate_b(2)],
        out_specs=pl.BlockSpec((tm, tn), lambda i, j: (i, j)),
        out_shape=jax.ShapeDtypeStruct((t, d), BF16),
        compiler_params=_params("parallel", "parallel"),
        name="gated_merge",
    )(att, conv, rw, gd, w_att_o, w_conv_o, w_rwkv_o, w_gate_up, w_gate_up, w_gate_up, bg, bg, bg)


def _pad_rows(w, rows_before, total):
    return jnp.pad(w, ((rows_before, total - rows_before - w.shape[0]), (0, 0)))


def kernel(x, norm_mix, w_in, w_att_o, conv_w, w_conv_o, rwkv_mu, rwkv_w0, rwkv_w_decay_up, rwkv_a0, rwkv_w_a_up, rwkv_w_g_up, rwkv_k_k, rwkv_k_a, rwkv_r_k, rwkv_lnx_w, rwkv_lnx_b, w_rwkv_o, w_gate_up, b_gate, w_out, norm_mlp, w_mlp_up, w_mlp_down, norm_final):
    batch, seq, d = x.shape
    depth = w_in.shape[0]
    t = batch * seq
    xs = x.reshape(t, d)
    att_end = 3 * ATT_W
    conv_end = att_end + 3 * CONV_W
    rw_end = conv_end + 3 * RWKV_W + LORA_BLOCK + GATE_LORA
    gate_pad = GATE_LORA_PAD - GATE_LORA
    for l in range(depth):
        wl = w_in[l].astype(BF16)
        w_rw = jnp.pad(wl[:, conv_end:rw_end], ((0, 0), (0, gate_pad)))
        mu = jnp.pad(rwkv_mu[l], (0, gate_pad))
        wd = _pad_rows(rwkv_w_decay_up[l], 0, LORA_BLOCK)
        wa = _pad_rows(rwkv_w_a_up[l], DECAY_LORA, LORA_BLOCK)
        wg = _pad_rows(rwkv_w_g_up[l], 0, GATE_LORA_PAD)

        h = rmsnorm(xs, norm_mix[l], BF16)
        p_att = matmul(h, wl[:, :att_end], BF16)
        p_conv = matmul(h, wl[:, att_end:conv_end], F32)
        p_rw = matmul(h, w_rw, F32, tn=1152)
        gd = matmul(h, wl[:, rw_end:], BF16)

        y_att = stick_breaking_attention(p_att, batch, seq)
        y_conv = short_gated_conv(p_conv, conv_w[l], seq)
        r, lw, k, v, a, b, g = rwkv_prep(p_rw, mu, rwkv_w0[l], wd, rwkv_a0[l], wa, wg,
                                         rwkv_k_k[l], rwkv_k_a[l], seq)
        y_rw = rwkv_scan(r, lw, k, v, a, b, g, rwkv_r_k[l], rwkv_lnx_w[l], rwkv_lnx_b[l], batch, seq)

        merged = gated_merge(y_att, y_conv, y_rw, gd, w_att_o[l].astype(BF16), w_conv_o[l].astype(BF16),
                             w_rwkv_o[l].astype(BF16), w_gate_up[l].astype(BF16), b_gate[l])
        xs = matmul(merged, w_out[l].astype(BF16), F32, epilogue="residual", res=xs)

        h = rmsnorm(xs, norm_mlp[l], BF16)
        up = matmul(h, w_mlp_up[l].astype(BF16), BF16, epilogue="relu2")
        xs = matmul(up, w_mlp_down[l].astype(BF16), F32, epilogue="residual", res=xs)
    return rmsnorm(xs, norm_final, F32).reshape(batch, seq, d)
```

```python
import functools

import jax
import jax.numpy as jnp
from jax import lax
from jax.experimental import pallas as pl
from jax.experimental.pallas import tpu as pltpu

F32 = jnp.float32
BF16 = jnp.bfloat16
HIGHEST = lax.Precision.HIGHEST

LANES = 128
SUBLANES = 8
VMEM_LIMIT_BYTES = 56 * 1024 * 1024

ATT_HEADS = 8
ATT_HEAD_DIM = 128
ATT_W = ATT_HEADS * ATT_HEAD_DIM
CONV_W = 1024
CONV_K = 3
RWKV_HEADS = 16
RWKV_HEAD_DIM = 64
RWKV_W = RWKV_HEADS * RWKV_HEAD_DIM
DECAY_LORA = 64
AAA_LORA = 64
GATE_LORA = 160
GATE_LORA_PAD = 256
LORA_BLOCK = DECAY_LORA + AAA_LORA
RWKV_COLS = 3 * RWKV_W + LORA_BLOCK + GATE_LORA_PAD
MERGE_RANK = 256
N_BRANCH = 3
RMS_EPS = 1e-6
LNX_EPS = 64e-5
KK_EPS = 1e-12
RWKV_CHUNK = 64
ATT_QUERY_BLOCK = 1024
ATT_KEY_BLOCK = 512


def _tile(dim, target, mult):
    best = None
    t = mult
    while t <= min(dim, target):
        if dim % t == 0:
            best = t
        t += mult
    return best if best is not None else dim


def _params(*sem):
    return pltpu.CompilerParams(dimension_semantics=sem, vmem_limit_bytes=VMEM_LIMIT_BYTES)


def _rmsnorm_kernel(x_ref, g_ref, o_ref):
    x = x_ref[...]
    ms = jnp.mean(x * x, axis=-1, keepdims=True)
    o_ref[...] = (x * lax.rsqrt(ms + RMS_EPS) * g_ref[...]).astype(o_ref.dtype)


def rmsnorm(x, g, out_dtype):
    t, d = x.shape
    tm = _tile(t, 256, SUBLANES)
    return pl.pallas_call(
        _rmsnorm_kernel,
        grid=(t // tm,),
        in_specs=[pl.BlockSpec((tm, d), lambda i: (i, 0)),
                  pl.BlockSpec((1, d), lambda i: (0, 0))],
        out_specs=pl.BlockSpec((tm, d), lambda i: (i, 0)),
        out_shape=jax.ShapeDtypeStruct((t, d), out_dtype),
        compiler_params=_params("parallel"),
        name="rmsnorm",
    )(x, g.reshape(1, d))


def _mm_epilogue(acc, epilogue, extra_ref):
    if epilogue == "relu2":
        acc = jnp.square(jnp.maximum(acc, 0.0))
    elif epilogue == "residual":
        acc = acc + extra_ref[...]
    elif epilogue == "colscale":
        acc = acc * extra_ref[...]
    return acc


def _mm_kernel(*refs, epilogue, nk):
    a_ref, w_ref = refs[:2]
    extra_ref = refs[2] if epilogue in ("residual", "colscale") else None
    o_ref = refs[3] if extra_ref is not None else refs[2]
    if nk == 1:
        acc = jnp.dot(a_ref[...], w_ref[0], preferred_element_type=F32)
        o_ref[...] = _mm_epilogue(acc, epilogue, extra_ref).astype(o_ref.dtype)
        return
    k = pl.program_id(2)
    if epilogue == "residual" and o_ref.dtype == F32:
        @pl.when(k == 0)
        def _():
            o_ref[...] = extra_ref[...] + jnp.dot(a_ref[...], w_ref[0], preferred_element_type=F32)

        @pl.when(k > 0)
        def _():
            o_ref[...] += jnp.dot(a_ref[...], w_ref[0], preferred_element_type=F32)
        return
    acc_ref = refs[-1]

    @pl.when(k == 0)
    def _():
        acc_ref[...] = jnp.zeros_like(acc_ref)

    acc_ref[...] += jnp.dot(a_ref[...], w_ref[0], preferred_element_type=F32)

    @pl.when(k == nk - 1)
    def _():
        o_ref[...] = _mm_epilogue(acc_ref[...], epilogue, extra_ref).astype(o_ref.dtype)


def matmul(a, w, layer, out_dtype, epilogue="none", extra=None, col0=0, ncols=None, tm=1024, tn=1024, tk=4096):
    m, kdim = a.shape
    n = w.shape[2] - col0 if ncols is None else ncols
    tm = _tile(m, tm, SUBLANES)
    tn = _tile(n, tn, LANES)
    assert col0 % tn == 0
    cb = col0 // tn
    tk = _tile(kdim, tk, LANES)
    nk = kdim // tk
    in_specs = [pl.BlockSpec((tm, tk), lambda i, j, k: (i, k)),
                pl.BlockSpec((1, tk, tn), lambda i, j, k: (layer, k, cb + j))]
    args = [a, w]
    if epilogue == "residual":
        in_specs.append(pl.BlockSpec((tm, tn), lambda i, j, k: (i, j)))
        args.append(extra)
    elif epilogue == "colscale":
        in_specs.append(pl.BlockSpec((1, tn), lambda i, j, k: (0, j)))
        args.append(extra)
    accumulate_in_output = epilogue == "residual" and out_dtype == F32
    return pl.pallas_call(
        functools.partial(_mm_kernel, epilogue=epilogue, nk=nk),
        grid=(m // tm, n // tn, nk),
        in_specs=in_specs,
        out_specs=pl.BlockSpec((tm, tn), lambda i, j, k: (i, j)),
        out_shape=jax.ShapeDtypeStruct((m, n), out_dtype),
        scratch_shapes=[pltpu.VMEM((tm, tn), F32)] if nk > 1 and not accumulate_in_output else [],
        compiler_params=_params("parallel", "parallel", "arbitrary"),
        name="matmul_" + epilogue,
    )(*args)


def _softplus(z):
    return jnp.maximum(z, 0.0) + jnp.log(1.0 + jnp.exp(-jnp.abs(z)))


def _att_kernel(q_ref, k_ref, v_ref, o_ref, acc_ref, carry_ref, *, tq, tk, grp):
    qi = pl.program_id(2)
    q = q_ref[...]
    ng = tk // grp
    diag_blocks = tq // tk
    r = lax.broadcasted_iota(jnp.int32, (grp, 2 * grp), 0)
    c = lax.broadcasted_iota(jnp.int32, (grp, 2 * grp), 1)
    suffix_and_total = jnp.where((c >= grp) | (r > c), 1.0, 0.0).astype(BF16)
    row = lax.broadcasted_iota(jnp.int32, (tq, grp), 0)
    col = lax.broadcasted_iota(jnp.int32, (tq, grp), 1)

    def sums_of(sp):
        return jnp.dot(sp.astype(BF16), suffix_and_total, preferred_element_type=F32)

    def visit(kstart, diag_offset):
        kb = k_ref[pl.ds(kstart, tk), :]
        vb = v_ref[pl.ds(kstart, tk), :]
        z = lax.dot_general(q, kb, (((1,), (1,)), ((), ())), preferred_element_type=F32)
        zs = [z[:, g * grp:(g + 1) * grp] for g in range(ng)]
        sps = [_softplus(zg) for zg in zs]
        masked = diag_offset is not None
        if masked:
            causal = [col + (diag_offset + g * grp) < row for g in range(ng)]
            sums = [sums_of(jnp.where(cg, sp, 0.0)) for cg, sp in zip(causal, sps)]
        else:
            sums = [sums_of(sp) for sp in sps]
        carry = carry_ref[...]
        ws = [None] * ng
        for g in reversed(range(ng)):
            w = jnp.exp(zs[g] - sps[g] - sums[g][:, :grp] - carry)
            if masked:
                w = jnp.where(causal[g], w, 0.0)
            ws[g] = w.astype(BF16)
            carry = carry + sums[g][:, grp:]
        carry_ref[...] = carry
        acc_ref[...] += jnp.dot(jnp.concatenate(ws, axis=1), vb, preferred_element_type=F32)

    acc_ref[...] = jnp.zeros_like(acc_ref)
    carry_ref[...] = jnp.zeros_like(carry_ref)
    for d in reversed(range(diag_blocks)):
        visit(pl.multiple_of(qi * tq + d * tk, tk), d * tk)

    def body(j, carry):
        visit(pl.multiple_of((qi * diag_blocks - 1 - j) * tk, tk), None)
        return carry

    lax.fori_loop(0, qi * diag_blocks, body, 0)
    o_ref[...] = acc_ref[...].astype(o_ref.dtype)


def stick_breaking_attention(p_att, batch, seq):
    tk = _tile(seq, ATT_KEY_BLOCK, LANES)
    tq = _tile(seq, ATT_QUERY_BLOCK, tk)
    nq = seq // tq
    d = ATT_HEAD_DIM
    return pl.pallas_call(
        functools.partial(_att_kernel, tq=tq, tk=tk, grp=LANES),
        grid=(batch, ATT_HEADS, nq),
        in_specs=[pl.BlockSpec((tq, d), lambda b, h, i: (b * nq + i, h)),
                  pl.BlockSpec((seq, d), lambda b, h, i: (b, ATT_HEADS + h)),
                  pl.BlockSpec((seq, d), lambda b, h, i: (b, 2 * ATT_HEADS + h))],
        out_specs=pl.BlockSpec((tq, d), lambda b, h, i: (b * nq + i, h)),
        out_shape=jax.ShapeDtypeStruct((batch * seq, ATT_W), BF16),
        scratch_shapes=[pltpu.VMEM((tq, d), F32), pltpu.VMEM((tq, LANES), F32)],
        compiler_params=_params("parallel", "parallel", "arbitrary"),
        name="stick_breaking_attention",
    )(p_att, p_att, p_att)


def _shift_rows(x, halo, n):
    row = lax.broadcasted_iota(jnp.int32, x.shape, 0)
    out = pltpu.roll(x, n, 0)
    for r in range(n):
        out = jnp.where(row == r, halo[SUBLANES - n + r:SUBLANES - n + r + 1, :], out)
    return out


def _conv_kernel(p_ref, halo_ref, w_ref, o_ref, *, tiles_per_seq):
    i = pl.program_id(0)
    first = (i % tiles_per_seq) == 0
    cw = CONV_W
    x = p_ref[:, cw:2 * cw] * p_ref[:, 2 * cw:3 * cw]
    hx = halo_ref[:, cw:2 * cw] * halo_ref[:, 2 * cw:3 * cw]
    hx = jnp.where(first, 0.0, hx)
    x1 = _shift_rows(x, hx, 1)
    x2 = _shift_rows(x, hx, 2)
    y = w_ref[0:1, :] * x2 + w_ref[1:2, :] * x1 + w_ref[2:3, :] * x
    o_ref[...] = (p_ref[:, 0:cw] * y).astype(o_ref.dtype)


def short_gated_conv(p_conv, conv_w, seq):
    t = p_conv.shape[0]
    ts = _tile(seq, 512, SUBLANES)
    hb = ts // SUBLANES
    return pl.pallas_call(
        functools.partial(_conv_kernel, tiles_per_seq=seq // ts),
        grid=(t // ts,),
        in_specs=[pl.BlockSpec((ts, 3 * CONV_W), lambda i: (i, 0)),
                  pl.BlockSpec((SUBLANES, 3 * CONV_W), lambda i: (jnp.maximum(i * hb - 1, 0), 0)),
                  pl.BlockSpec((CONV_K, CONV_W), lambda i: (0, 0))],
        out_specs=pl.BlockSpec((ts, CONV_W), lambda i: (i, 0)),
        out_shape=jax.ShapeDtypeStruct((t, CONV_W), BF16),
        compiler_params=_params("parallel"),
        name="short_gated_conv",
    )(p_conv, p_conv, conv_w)


def _head_sum_matrix(n, value):
    r = lax.broadcasted_iota(jnp.int32, (n, n), 0) // RWKV_HEAD_DIM
    c = lax.broadcasted_iota(jnp.int32, (n, n), 1) // RWKV_HEAD_DIM
    return jnp.where(r == c, value, 0.0).astype(F32)


def _rw_prep_kernel(p_ref, halo_ref, mu_ref, w0_ref, wd_ref, a0_ref, wa_ref, wg_ref, kk_ref, ka_ref,
                    r_ref, cum_ref, k_ref, v_ref, a_ref, b_ref, g_ref, *, tiles_per_seq, chunk):
    i = pl.program_id(0)
    first = (i % tiles_per_seq) == 0
    p = p_ref[...]
    halo = jnp.where(first, 0.0, halo_ref[...])
    prev = _shift_rows(p, halo, 1)
    seg = p + mu_ref[...] * (prev - p)
    w = RWKV_W
    r = seg[:, 0:w]
    k = seg[:, w:2 * w]
    v = seg[:, 2 * w:3 * w]
    lora = seg[:, 3 * w:3 * w + LORA_BLOCK]
    lg = seg[:, 3 * w + LORA_BLOCK:]
    wlog = w0_ref[...] + jnp.dot(jnp.tanh(lora), wd_ref[...], precision=HIGHEST, preferred_element_type=F32)
    wlog = -_softplus(-wlog) - 0.5
    lw = -jnp.exp(wlog)
    ts = p.shape[0]
    ti = lax.broadcasted_iota(jnp.int32, (ts, ts), 0)
    si = lax.broadcasted_iota(jnp.int32, (ts, ts), 1)
    in_chunk_prefix = ((si <= ti) & (si // chunk == ti // chunk)).astype(F32)
    cum_ref[...] = jnp.dot(in_chunk_prefix, lw, precision=HIGHEST, preferred_element_type=F32)
    ag = jax.nn.sigmoid(a0_ref[...] + jnp.dot(lora, wa_ref[...], precision=HIGHEST, preferred_element_type=F32))
    g_ref[...] = jnp.dot(jax.nn.sigmoid(lg), wg_ref[...], precision=HIGHEST, preferred_element_type=F32)
    kk = k * kk_ref[...]
    head_ones = _head_sum_matrix(LANES, 1.0)
    sq = kk * kk
    ss = jnp.concatenate(
        [jnp.dot(sq[:, c:c + LANES], head_ones, precision=HIGHEST, preferred_element_type=F32)
         for c in range(0, w, LANES)], axis=1)
    kk = kk / jnp.maximum(jnp.sqrt(ss), KK_EPS)
    r_ref[...] = r
    k_ref[...] = k * (1.0 + (ag - 1.0) * ka_ref[...])
    v_ref[...] = v
    a_ref[...] = -kk * jnp.exp(-lw)
    b_ref[...] = kk * ag


def rwkv_prep(p_rw, mu, w0, wd, a0, wa, wg, k_k, k_a, seq):
    t = p_rw.shape[0]
    ts = _tile(seq, 256, RWKV_CHUNK)
    assert ts % RWKV_CHUNK == 0
    hb = ts // SUBLANES
    w = RWKV_W
    row = lambda n: pl.BlockSpec((1, n), lambda i: (0, 0))
    full = lambda a: pl.BlockSpec(a.shape, lambda i: (0, 0))
    out = jax.ShapeDtypeStruct((t, w), F32)
    return pl.pallas_call(
        functools.partial(_rw_prep_kernel, tiles_per_seq=seq // ts, chunk=RWKV_CHUNK),
        grid=(t // ts,),
        in_specs=[pl.BlockSpec((ts, RWKV_COLS), lambda i: (i, 0)),
                  pl.BlockSpec((SUBLANES, RWKV_COLS), lambda i: (jnp.maximum(i * hb - 1, 0), 0)),
                  row(RWKV_COLS), row(w), full(wd), row(w), full(wa), full(wg), row(w), row(w)],
        out_specs=[pl.BlockSpec((ts, w), lambda i: (i, 0))] * 7,
        out_shape=[out] * 7,
        compiler_params=_params("parallel"),
        name="rwkv_prep",
    )(p_rw, p_rw, mu.reshape(1, -1), w0.reshape(1, w), wd, a0.reshape(1, w), wa, wg,
      k_k.reshape(1, w), k_a.reshape(1, w))


def _split_bf16(x):
    bits = lax.bitcast_convert_type(x, jnp.uint32) & jnp.uint32(0xFFFF0000)
    hi = lax.bitcast_convert_type(bits, F32)
    return hi.astype(BF16), (x - hi).astype(BF16)


def _pdot(x, y, passes, nt=False):
    dims = (((1,), (1,)), ((), ())) if nt else (((1,), (0,)), ((), ()))
    dot = lambda p, q: lax.dot_general(p, q, dims, preferred_element_type=F32)
    if passes == 1:
        return dot(x.astype(BF16), y.astype(BF16))
    assert passes == 3
    xh, xl = _split_bf16(x)
    yh, yl = _split_bf16(y)
    return dot(xh, yh) + dot(xh, yl) + dot(xl, yh)


RW_PASSES = dict(m=1, inv=1, apply=1, gh=1, state=3, norm=1)


def _rw_chunk(rs, cums, ks, vs, a_tils, bs, zs, c, ps):
    n = 2 * RWKV_HEAD_DIM
    row = lax.broadcasted_iota(jnp.int32, (c, n), 0)
    lane = lax.broadcasted_iota(jnp.int32, (c, n), 1)
    lane_a = lane < RWKV_HEAD_DIM
    src = jnp.where(lane_a, lane, lane - RWKV_HEAD_DIM)
    strict = src < row
    incl = src <= row
    sq_r = lax.broadcasted_iota(jnp.int32, (n, n), 0)
    sq_c = lax.broadcasted_iota(jnp.int32, (n, n), 1)
    same_head = (sq_r < RWKV_HEAD_DIM) == (sq_c < RWKV_HEAD_DIM)
    eye = sq_r == sq_c
    each = lambda f, *lists: [f(*xs) for xs in zip(*lists)]
    cat = lambda *xs: jnp.concatenate(xs, axis=0)
    only_a = lambda x: jnp.where(lane_a, x, 0.0)
    only_b = lambda x: jnp.where(lane_a, 0.0, x)

    def fold(stacked):
        kept = jnp.where(same_head, stacked, 0.0)
        return kept[:c] + kept[c:]

    cum_ends = each(lambda cum: cum[c - 1:c, :], cums)
    grows = each(jnp.exp, cums)
    invs = each(lambda cum: jnp.exp(-cum), cums)
    to_ends = each(lambda cum, ce: jnp.exp(ce - cum), cums, cum_ends)
    a_hats = each(jnp.multiply, a_tils, grows)
    r_hats = each(jnp.multiply, rs, grows)
    b_chks = each(jnp.multiply, bs, invs)
    k_chks = each(jnp.multiply, ks, invs)
    b_ends = each(jnp.multiply, bs, to_ends)
    k_ends = each(jnp.multiply, ks, to_ends)

    m_as = each(lambda ah, rh, bc, kc: _pdot(cat(only_a(ah), only_a(rh)), cat(bc, kc), ps["m"], nt=True),
                a_hats, r_hats, b_chks, k_chks)
    m_bs = each(lambda ah, rh, bc, kc: _pdot(cat(only_b(ah), only_b(rh)), cat(kc, bc), ps["m"], nt=True),
                a_hats, r_hats, b_chks, k_chks)
    ma_tops = each(lambda m: jnp.where(strict, m[:c], 0.0), m_as)
    ma_bots = each(lambda m: jnp.where(incl, m[c:], 0.0), m_as)
    mb_tops = each(lambda m: jnp.where(strict, m[:c], 0.0), m_bs)
    mb_bots = each(lambda m: jnp.where(incl, m[c:], 0.0), m_bs)

    powers = each(lambda ta, tb: cat(only_a(ta), only_b(tb)), ma_tops, mb_tops)
    t_invs = each(lambda nil: jnp.where(eye, 1.0, 0.0) + nil, powers)
    for _ in range(c.bit_length() - 2):
        powers = each(lambda pw: _pdot(pw, pw, ps["inv"]), powers)
        t_invs = each(lambda ti, pw: ti + _pdot(ti, pw, ps["inv"]), t_invs, powers)

    a_news = each(lambda ti, ah: fold(_pdot(ti, cat(only_a(ah), only_b(ah)), ps["apply"])), t_invs, a_hats)
    w1s = each(lambda ta, tb, v: jnp.where(same_head, _pdot(cat(only_b(ta), only_a(tb)), cat(v, v), ps["apply"]), 0.0),
               ma_tops, mb_tops, vs)
    v_news = each(lambda ti, w1: fold(_pdot(ti, w1, ps["apply"])), t_invs, w1s)
    r_news = each(lambda rh, ba, bb, an: rh + fold(_pdot(cat(only_a(ba), only_b(bb)), cat(an, an), ps["apply"])),
                  r_hats, ma_bots, mb_bots, a_news)
    y_ins = each(lambda ba, bb, vn, v: jnp.where(lane_a, _pdot(ba, cat(vn, v), ps["apply"]),
                                                 _pdot(bb, cat(v, vn), ps["apply"])),
                 ma_bots, mb_bots, v_news, vs)
    g_mats = each(lambda be, an, ce: (jnp.where(same_head, _pdot(be.T, an, ps["gh"]), 0.0)
                                      + jnp.where(eye, jnp.broadcast_to(jnp.exp(ce), (n, n)), 0.0)),
                  b_ends, a_news, cum_ends)
    h_mats = each(lambda be, ke, vn, v: jnp.where(same_head, _pdot(cat(be, ke).T, cat(vn, v), ps["gh"]), 0.0),
                  b_ends, k_ends, v_news, vs)
    ys = each(lambda rn, z, yi: _pdot(rn, z, ps["state"]) + yi, r_news, zs, y_ins)
    z_news = each(lambda g, z, h: _pdot(g, z, ps["state"]) + h, g_mats, zs, h_mats)
    return ys, z_news


def _rw_scan_kernel(r_ref, cum_ref, k_ref, v_ref, a_ref, b_ref, g_ref, rk_ref, lnw_ref, lnb_ref,
                    o_ref, z_ref, *, chunk, pairs, passes):
    n = 2 * RWKV_HEAD_DIM
    assert n == LANES and 2 * chunk == n

    @pl.when(pl.program_id(2) == 0)
    def _():
        z_ref[...] = jnp.zeros_like(z_ref)

    sq_r = lax.broadcasted_iota(jnp.int32, (n, n), 0)
    sq_c = lax.broadcasted_iota(jnp.int32, (n, n), 1)
    same_head = (sq_r < RWKV_HEAD_DIM) == (sq_c < RWKV_HEAD_DIM)
    head_mean = jnp.where(same_head, 1.0 / RWKV_HEAD_DIM, 0.0).astype(F32)
    head_ones = jnp.where(same_head, 1.0, 0.0).astype(F32)
    sls = [slice(p * n, (p + 1) * n) for p in range(pairs)]
    take = lambda ref: [ref[:, sl] for sl in sls]
    rs, ks, vs = take(r_ref), take(k_ref), take(v_ref)
    ys, z_news = _rw_chunk(rs, take(cum_ref), ks, vs, take(a_ref), take(b_ref),
                           [z_ref[p] for p in range(pairs)], chunk, passes)
    for p in range(pairs):
        z_ref[p] = z_news[p]
    means = [_pdot(y, head_mean, passes["norm"]) for y in ys]
    devs = [y - m for y, m in zip(ys, means)]
    variances = [_pdot(d * d, head_mean, passes["norm"]) for d in devs]
    bonuses = [_pdot(r * k * rk_ref[:, sl], head_ones, passes["norm"]) * v for r, k, v, sl in zip(rs, ks, vs, sls)]
    for d, var, bonus, sl in zip(devs, variances, bonuses, sls):
        yn = d * lax.rsqrt(var + LNX_EPS) * lnw_ref[:, sl] + lnb_ref[:, sl]
        o_ref[:, sl] = ((yn + bonus) * g_ref[:, sl]).astype(o_ref.dtype)


def rwkv_scan(r, cum, k, v, a_til, b, g, r_k, lnx_w, lnx_b, batch, seq, pairs_per_step=8):
    t = r.shape[0]
    c = RWKV_CHUNK
    nc = seq // c
    n = 2 * RWKV_HEAD_DIM
    pp = pairs_per_step
    groups = RWKV_W // (n * pp)
    tok = pl.BlockSpec((c, n * pp), lambda bi, p, ci: (bi * nc + ci, p))
    par = pl.BlockSpec((1, n * pp), lambda bi, p, ci: (0, p))
    return pl.pallas_call(
        functools.partial(_rw_scan_kernel, chunk=c, pairs=pp, passes=dict(RW_PASSES)),
        grid=(batch, groups, nc),
        in_specs=[tok] * 7 + [par] * 3,
        out_specs=tok,
        out_shape=jax.ShapeDtypeStruct((t, RWKV_W), BF16),
        scratch_shapes=[pltpu.VMEM((pp, n, n), F32)],
        compiler_params=_params("parallel", "parallel", "arbitrary"),
        name="rwkv_scan",
    )(r, cum, k, v, a_til, b, g, r_k.reshape(1, RWKV_W), lnx_w.reshape(1, RWKV_W), lnx_b.reshape(1, RWKV_W))


def _merge_kernel(att_ref, conv_ref, rw_ref, gd_ref, wa_ref, wc_ref, wr_ref,
                  ga_ref, gc_ref, gr_ref, ba_ref, bc_ref, br_ref, o_ref):
    gd = gd_ref[...]

    def branch(x_ref, w_ref, gw_ref, gb_ref):
        y = jnp.dot(x_ref[...], w_ref[0], preferred_element_type=F32)
        gate = jax.nn.sigmoid(jnp.dot(gd, gw_ref[0], preferred_element_type=F32) + gb_ref[0])
        return gate * y

    out = branch(att_ref, wa_ref, ga_ref, ba_ref)
    out = out + branch(conv_ref, wc_ref, gc_ref, bc_ref)
    out = out + branch(rw_ref, wr_ref, gr_ref, br_ref)
    o_ref[...] = out.astype(o_ref.dtype)


def gated_merge(att, conv, rw, gd, w_att_o, w_conv_o, w_rwkv_o, w_gate_up, b_gate, layer):
    t = att.shape[0]
    d = w_att_o.shape[2]
    tm = _tile(t, 1024, SUBLANES)
    tn = _tile(d, 512, LANES)
    nj = d // tn
    act = lambda w: pl.BlockSpec((tm, w), lambda i, j: (i, 0))
    wout = lambda w: pl.BlockSpec((1, w, tn), lambda i, j: (layer, 0, j))
    gate_w = lambda br: pl.BlockSpec((1, MERGE_RANK, tn), lambda i, j: (layer, 0, br * nj + j))
    gate_b = lambda br: pl.BlockSpec((1, 1, tn), lambda i, j: (layer, 0, br * nj + j))
    bg = b_gate.reshape(b_gate.shape[0], 1, N_BRANCH * d)
    return pl.pallas_call(
        _merge_kernel,
        grid=(t // tm, nj),
        in_specs=[act(ATT_W), act(CONV_W), act(RWKV_W), act(MERGE_RANK),
                  wout(ATT_W), wout(CONV_W), wout(RWKV_W),
                  gate_w(0), gate_w(1), gate_w(2), gate_b(0), gate_b(1), gate_b(2)],
        out_specs=pl.BlockSpec((tm, tn), lambda i, j: (i, j)),
        out_shape=jax.ShapeDtypeStruct((t, d), BF16),
        compiler_params=_params("parallel", "parallel"),
        name="gated_merge",
    )(att, conv, rw, gd, w_att_o, w_conv_o, w_rwkv_o, w_gate_up, w_gate_up, w_gate_up, bg, bg, bg)


def _pad_rows(w, rows_before, total):
    return jnp.pad(w, ((rows_before, total - rows_before - w.shape[0]), (0, 0)))


def kernel(x, norm_mix, w_in, w_att_o, conv_w, w_conv_o, rwkv_mu, rwkv_w0, rwkv_w_decay_up, rwkv_a0, rwkv_w_a_up, rwkv_w_g_up, rwkv_k_k, rwkv_k_a, rwkv_r_k, rwkv_lnx_w, rwkv_lnx_b, w_rwkv_o, w_gate_up, b_gate, w_out, norm_mlp, w_mlp_up, w_mlp_down, norm_final):
    batch, seq, d = x.shape
    depth = w_in.shape[0]
    t = batch * seq
    xs = x.reshape(t, d)
    att_end = 3 * ATT_W
    conv_end = att_end + 3 * CONV_W
    rw_end = conv_end + 3 * RWKV_W + LORA_BLOCK + GATE_LORA
    gate_pad = GATE_LORA_PAD - GATE_LORA
    w_in_b = w_in.astype(BF16)
    w_rw_b = jnp.pad(w_in_b[:, :, conv_end:rw_end], ((0, 0), (0, 0), (0, gate_pad)))
    w_gd_b = w_in_b[:, :, rw_end:]
    w_att_o_b, w_conv_o_b, w_rwkv_o_b = w_att_o.astype(BF16), w_conv_o.astype(BF16), w_rwkv_o.astype(BF16)
    w_gate_up_b, w_out_b = w_gate_up.astype(BF16), w_out.astype(BF16)
    w_mlp_up_b, w_mlp_down_b = w_mlp_up.astype(BF16), w_mlp_down.astype(BF16)
    q_scale = jnp.concatenate([jnp.full((1, ATT_W), ATT_HEAD_DIM ** -0.5, F32), jnp.ones((1, 2 * ATT_W), F32)], axis=1)
    for l in range(depth):
        mu = jnp.pad(rwkv_mu[l], (0, gate_pad))
        wd = _pad_rows(rwkv_w_decay_up[l], 0, LORA_BLOCK)
        wa = _pad_rows(rwkv_w_a_up[l], DECAY_LORA, LORA_BLOCK)
        wg = _pad_rows(rwkv_w_g_up[l], 0, GATE_LORA_PAD)

        h = rmsnorm(xs, norm_mix[l], BF16)
        p_att = matmul(h, w_in_b, l, BF16, epilogue="colscale", extra=q_scale, ncols=att_end)
        p_conv = matmul(h, w_in_b, l, F32, col0=att_end, ncols=conv_end - att_end)
        p_rw = matmul(h, w_rw_b, l, F32, tn=1152)
        gd = matmul(h, w_gd_b, l, BF16)

        y_att = stick_breaking_attention(p_att, batch, seq)
        y_conv = short_gated_conv(p_conv, conv_w[l], seq)
        r, cum, k, v, a_til, b, g = rwkv_prep(p_rw, mu, rwkv_w0[l], wd, rwkv_a0[l], wa, wg,
                                             rwkv_k_k[l], rwkv_k_a[l], seq)
        y_rw = rwkv_scan(r, cum, k, v, a_til, b, g, rwkv_r_k[l], rwkv_lnx_w[l], rwkv_lnx_b[l], batch, seq)

        merged = gated_merge(y_att, y_conv, y_rw, gd, w_att_o_b, w_conv_o_b, w_rwkv_o_b, w_gate_up_b, b_gate, l)
        xs = matmul(merged, w_out_b, l, F32, epilogue="residual", extra=xs)

        h = rmsnorm(xs, norm_mlp[l], BF16)
        up = matmul(h, w_mlp_up_b, l, BF16, epilogue="relu2")
        xs = matmul(up, w_mlp_down_b, l, F32, epilogue="residual", extra=xs, tk=2048)
    return rmsnorm(xs, norm_final, F32).reshape(batch, seq, d)
```

```python
import functools

import jax
import jax.numpy as jnp
from jax import lax
from jax.experimental import pallas as pl
from jax.experimental.pallas import tpu as pltpu

F32 = jnp.float32
BF16 = jnp.bfloat16

LANES = 128
SUBLANES = 8
VMEM_LIMIT_BYTES = 56 * 1024 * 1024

ATT_HEADS = 8
ATT_HEAD_DIM = 128
ATT_W = ATT_HEADS * ATT_HEAD_DIM
CONV_W = 1024
CONV_K = 3
RWKV_HEADS = 16
RWKV_HEAD_DIM = 64
RWKV_W = RWKV_HEADS * RWKV_HEAD_DIM
DECAY_LORA = 64
AAA_LORA = 64
GATE_LORA = 160
GATE_LORA_PAD = 256
LORA_BLOCK = DECAY_LORA + AAA_LORA
RWKV_COLS = 3 * RWKV_W + LORA_BLOCK + GATE_LORA_PAD
MERGE_RANK = 256
N_BRANCH = 3
RMS_EPS = 1e-6
LNX_EPS = 64e-5
KK_EPS = 1e-12
RWKV_CHUNK = 64
ATT_QUERY_BLOCK = 1024
ATT_KEY_BLOCK = 1024
ATT_GROUP = 256
LOG2_E = 1.4426950408889634


def _tile(dim, target, mult):
    best = None
    t = mult
    while t <= min(dim, target):
        if dim % t == 0:
            best = t
        t += mult
    return best if best is not None else dim


def _params(*sem):
    return pltpu.CompilerParams(dimension_semantics=sem, vmem_limit_bytes=VMEM_LIMIT_BYTES)


def _rmsnorm_kernel(x_ref, g_ref, o_ref):
    x = x_ref[...]
    ms = jnp.mean(x * x, axis=-1, keepdims=True)
    o_ref[...] = (x * lax.rsqrt(ms + RMS_EPS) * g_ref[...]).astype(o_ref.dtype)


def rmsnorm(x, g, out_dtype):
    t, d = x.shape
    tm = _tile(t, 256, SUBLANES)
    return pl.pallas_call(
        _rmsnorm_kernel,
        grid=(t // tm,),
        in_specs=[pl.BlockSpec((tm, d), lambda i: (i, 0)),
                  pl.BlockSpec((1, d), lambda i: (0, 0))],
        out_specs=pl.BlockSpec((tm, d), lambda i: (i, 0)),
        out_shape=jax.ShapeDtypeStruct((t, d), out_dtype),
        compiler_params=_params("parallel"),
        name="rmsnorm",
    )(x, g.reshape(1, d))


def _mm_epilogue(acc, epilogue, extra_ref):
    if epilogue == "relu2":
        acc = jnp.square(jnp.maximum(acc, 0.0))
    elif epilogue == "residual":
        acc = acc + extra_ref[...]
    elif epilogue == "colscale":
        acc = acc * extra_ref[...]
    return acc


def _mm_kernel(*refs, epilogue, nk):
    a_ref, w_ref = refs[:2]
    extra_ref = refs[2] if epilogue in ("residual", "colscale") else None
    o_ref = refs[3] if extra_ref is not None else refs[2]
    if nk == 1:
        acc = jnp.dot(a_ref[...], w_ref[0], preferred_element_type=F32)
        o_ref[...] = _mm_epilogue(acc, epilogue, extra_ref).astype(o_ref.dtype)
        return
    k = pl.program_id(2)
    if epilogue == "residual" and o_ref.dtype == F32:
        @pl.when(k == 0)
        def _():
            o_ref[...] = extra_ref[...] + jnp.dot(a_ref[...], w_ref[0], preferred_element_type=F32)

        @pl.when(k > 0)
        def _():
            o_ref[...] += jnp.dot(a_ref[...], w_ref[0], preferred_element_type=F32)
        return
    acc_ref = refs[-1]

    @pl.when(k == 0)
    def _():
        acc_ref[...] = jnp.zeros_like(acc_ref)

    acc_ref[...] += jnp.dot(a_ref[...], w_ref[0], preferred_element_type=F32)

    @pl.when(k == nk - 1)
    def _():
        o_ref[...] = _mm_epilogue(acc_ref[...], epilogue, extra_ref).astype(o_ref.dtype)


def matmul(a, w, layer, out_dtype, epilogue="none", extra=None, col0=0, ncols=None, tm=1024, tn=1024, tk=4096):
    m, kdim = a.shape
    n = w.shape[2] - col0 if ncols is None else ncols
    tm = _tile(m, tm, SUBLANES)
    tn = _tile(n, tn, LANES)
    assert col0 % tn == 0
    cb = col0 // tn
    tk = _tile(kdim, tk, LANES)
    nk = kdim // tk
    in_specs = [pl.BlockSpec((tm, tk), lambda i, j, k: (i, k)),
                pl.BlockSpec((1, tk, tn), lambda i, j, k: (layer, k, cb + j))]
    args = [a, w]
    if epilogue == "residual":
        in_specs.append(pl.BlockSpec((tm, tn), lambda i, j, k: (i, j)))
        args.append(extra)
    elif epilogue == "colscale":
        in_specs.append(pl.BlockSpec((1, tn), lambda i, j, k: (0, j)))
        args.append(extra)
    accumulate_in_output = epilogue == "residual" and out_dtype == F32
    return pl.pallas_call(
        functools.partial(_mm_kernel, epilogue=epilogue, nk=nk),
        grid=(m // tm, n // tn, nk),
        in_specs=in_specs,
        out_specs=pl.BlockSpec((tm, tn), lambda i, j, k: (i, j)),
        out_shape=jax.ShapeDtypeStruct((m, n), out_dtype),
        scratch_shapes=[pltpu.VMEM((tm, tn), F32)] if nk > 1 and not accumulate_in_output else [],
        compiler_params=_params("parallel", "parallel", "arbitrary"),
        name="matmul_" + epilogue,
    )(*args)


def _softplus(z):
    return jnp.maximum(z, 0.0) + jnp.log(1.0 + jnp.exp(-jnp.abs(z)))


def _softplus_base2(z2):
    neg_abs = lax.bitcast_convert_type(lax.bitcast_convert_type(z2, jnp.uint32) | jnp.uint32(0x80000000), F32)
    return jnp.maximum(z2, 0.0) + jnp.log(1.0 + jnp.exp2(neg_abs)) * LOG2_E


def _att_kernel(q_ref, k_ref, v_ref, o_ref, acc_ref, carry_ref, *, tq, tk, grp):
    qi = pl.program_id(2)
    q = q_ref[...]
    ng = tk // grp
    diag_blocks = tq // tk
    r = lax.broadcasted_iota(jnp.int32, (grp, grp), 0)
    c = lax.broadcasted_iota(jnp.int32, (grp, grp), 1)
    later = jnp.where(r > c, 1.0, 0.0).astype(BF16)
    row = lax.broadcasted_iota(jnp.int32, (tq, grp), 0)
    col = lax.broadcasted_iota(jnp.int32, (tq, grp), 1)

    def visit(kstart, diag_offset):
        kb = k_ref[pl.ds(kstart, tk), :]
        vb = v_ref[pl.ds(kstart, tk), :]
        z = lax.dot_general(q, kb, (((1,), (1,)), ((), ())), preferred_element_type=F32)
        zs = [z[:, g * grp:(g + 1) * grp] for g in range(ng)]
        sps = [_softplus_base2(zg) for zg in zs]
        masked = diag_offset is not None
        if masked:
            causal = [col + (diag_offset + g * grp) < row for g in range(ng)]
            spms = [jnp.where(cg, sp, 0.0) for cg, sp in zip(causal, sps)]
        else:
            spms = sps
        afters = [jnp.dot(sp.astype(BF16), later, preferred_element_type=F32) for sp in spms]
        totals = [af[:, 0:1] + sp[:, 0:1] for af, sp in zip(afters, spms)]
        carry = carry_ref[...]
        ws = [None] * ng
        for g in reversed(range(ng)):
            w = jnp.exp2(zs[g] - sps[g] - afters[g] - carry)
            if masked:
                w = jnp.where(causal[g], w, 0.0)
            ws[g] = w.astype(BF16)
            carry = carry + totals[g]
        carry_ref[...] = carry
        acc_ref[...] += jnp.dot(jnp.concatenate(ws, axis=1), vb, preferred_element_type=F32)

    acc_ref[...] = jnp.zeros_like(acc_ref)
    carry_ref[...] = jnp.zeros_like(carry_ref)
    for d in reversed(range(diag_blocks)):
        visit(pl.multiple_of(qi * tq + d * tk, tk), d * tk)

    def body(j, carry):
        visit(pl.multiple_of((qi * diag_blocks - 1 - j) * tk, tk), None)
        return carry

    lax.fori_loop(0, qi * diag_blocks, body, 0)
    o_ref[...] = acc_ref[...].astype(o_ref.dtype)


def stick_breaking_attention(p_att, batch, seq):
    tk = _tile(seq, ATT_KEY_BLOCK, LANES)
    tq = _tile(seq, ATT_QUERY_BLOCK, tk)
    nq = seq // tq
    d = ATT_HEAD_DIM
    return pl.pallas_call(
        functools.partial(_att_kernel, tq=tq, tk=tk, grp=ATT_GROUP),
        grid=(batch, ATT_HEADS, nq),
        in_specs=[pl.BlockSpec((tq, d), lambda b, h, i: (b * nq + i, h)),
                  pl.BlockSpec((seq, d), lambda b, h, i: (b, ATT_HEADS + h)),
                  pl.BlockSpec((seq, d), lambda b, h, i: (b, 2 * ATT_HEADS + h))],
        out_specs=pl.BlockSpec((tq, d), lambda b, h, i: (b * nq + i, h)),
        out_shape=jax.ShapeDtypeStruct((batch * seq, ATT_W), BF16),
        scratch_shapes=[pltpu.VMEM((tq, d), F32), pltpu.VMEM((tq, 1), F32)],
        compiler_params=_params("parallel", "parallel", "arbitrary"),
        name="stick_breaking_attention",
    )(p_att, p_att, p_att)


def _shift_rows(x, halo, n):
    row = lax.broadcasted_iota(jnp.int32, x.shape, 0)
    out = pltpu.roll(x, n, 0)
    for r in range(n):
        out = jnp.where(row == r, halo[SUBLANES - n + r:SUBLANES - n + r + 1, :], out)
    return out


def _conv_kernel(p_ref, halo_ref, w_ref, o_ref, *, tiles_per_seq):
    i = pl.program_id(0)
    first = (i % tiles_per_seq) == 0
    cw = CONV_W
    x = p_ref[:, cw:2 * cw] * p_ref[:, 2 * cw:3 * cw]
    hx = halo_ref[:, cw:2 * cw] * halo_ref[:, 2 * cw:3 * cw]
    hx = jnp.where(first, 0.0, hx)
    x1 = _shift_rows(x, hx, 1)
    x2 = _shift_rows(x, hx, 2)
    y = w_ref[0:1, :] * x2 + w_ref[1:2, :] * x1 + w_ref[2:3, :] * x
    o_ref[...] = (p_ref[:, 0:cw] * y).astype(o_ref.dtype)


def short_gated_conv(p_conv, conv_w, seq):
    t = p_conv.shape[0]
    ts = _tile(seq, 512, SUBLANES)
    hb = ts // SUBLANES
    return pl.pallas_call(
        functools.partial(_conv_kernel, tiles_per_seq=seq // ts),
        grid=(t // ts,),
        in_specs=[pl.BlockSpec((ts, 3 * CONV_W), lambda i: (i, 0)),
                  pl.BlockSpec((SUBLANES, 3 * CONV_W), lambda i: (jnp.maximum(i * hb - 1, 0), 0)),
                  pl.BlockSpec((CONV_K, CONV_W), lambda i: (0, 0))],
        out_specs=pl.BlockSpec((ts, CONV_W), lambda i: (i, 0)),
        out_shape=jax.ShapeDtypeStruct((t, CONV_W), BF16),
        compiler_params=_params("parallel"),
        name="short_gated_conv",
    )(p_conv, p_conv, conv_w)


def _trunc_bf16(x):
    bits = lax.bitcast_convert_type(x, jnp.uint32) & jnp.uint32(0xFFFF0000)
    hi = lax.bitcast_convert_type(bits, F32)
    return hi.astype(BF16), x - hi


def _bf16_pieces(x, terms):
    pieces = []
    for _ in range(terms - 1):
        piece, x = _trunc_bf16(x)
        pieces.append(piece)
    return pieces + [x.astype(BF16)]


def _pdot(x, y, passes, nt=False):
    dims = (((1,), (1,)), ((), ())) if nt else (((1,), (0,)), ((), ()))
    dot = lambda p, q: lax.dot_general(p, q, dims, preferred_element_type=F32)
    if passes == 1:
        return dot(x.astype(BF16), y.astype(BF16))
    assert passes == 3
    xh, xl = _bf16_pieces(x, 2)
    yh, yl = _bf16_pieces(y, 2)
    return dot(xh, yh) + dot(xh, yl) + dot(xl, yh)


def _dot_with_mask(x, y, terms, mask_side):
    if mask_side == "lhs":
        mask = x.astype(BF16)
        parts = [jnp.dot(mask, p, preferred_element_type=F32) for p in _bf16_pieces(y, terms)]
    else:
        mask = y.astype(BF16)
        parts = [jnp.dot(p, mask, preferred_element_type=F32) for p in _bf16_pieces(x, terms)]
    return functools.reduce(jnp.add, parts)


def _head_sum_matrix(n, value):
    r = lax.broadcasted_iota(jnp.int32, (n, n), 0) // RWKV_HEAD_DIM
    c = lax.broadcasted_iota(jnp.int32, (n, n), 1) // RWKV_HEAD_DIM
    return jnp.where(r == c, value, 0.0).astype(F32)


def _rw_prep_kernel(p_ref, halo_ref, mu_ref, w0_ref, wd_ref, a0_ref, wa_ref, wg_ref, kk_ref, ka_ref,
                    r_ref, cum_ref, k_ref, v_ref, a_ref, b_ref, g_ref, *, tiles_per_seq, chunk):
    i = pl.program_id(0)
    first = (i % tiles_per_seq) == 0
    p = p_ref[...]
    halo = jnp.where(first, 0.0, halo_ref[...])
    prev = _shift_rows(p, halo, 1)
    seg = p + mu_ref[...] * (prev - p)
    w = RWKV_W
    r = seg[:, 0:w]
    k = seg[:, w:2 * w]
    v = seg[:, 2 * w:3 * w]
    lora = seg[:, 3 * w:3 * w + LORA_BLOCK]
    lg = seg[:, 3 * w + LORA_BLOCK:]
    wlog = w0_ref[...] + _pdot(jnp.tanh(lora), wd_ref[...], 3)
    wlog = -_softplus(-wlog) - 0.5
    lw = -jnp.exp(wlog)
    ts = p.shape[0]
    ti = lax.broadcasted_iota(jnp.int32, (ts, ts), 0)
    si = lax.broadcasted_iota(jnp.int32, (ts, ts), 1)
    in_chunk_prefix = jnp.where((si <= ti) & (si // chunk == ti // chunk), 1.0, 0.0)
    cum_ref[...] = _dot_with_mask(in_chunk_prefix, lw, 3, "lhs")
    ag = jax.nn.sigmoid(a0_ref[...] + _pdot(lora, wa_ref[...], 3))
    g_ref[...] = _pdot(jax.nn.sigmoid(lg), wg_ref[...], 3)
    kk = k * kk_ref[...]
    head_ones = _head_sum_matrix(LANES, 1.0)
    sq = kk * kk
    ss = jnp.concatenate(
        [_dot_with_mask(sq[:, c:c + LANES], head_ones, 2, "rhs") for c in range(0, w, LANES)], axis=1)
    kk = kk / jnp.maximum(jnp.sqrt(ss), KK_EPS)
    r_ref[...] = r
    k_ref[...] = k * (1.0 + (ag - 1.0) * ka_ref[...])
    v_ref[...] = v
    a_ref[...] = -kk * jnp.exp(-lw)
    b_ref[...] = kk * ag


def rwkv_prep(p_rw, mu, w0, wd, a0, wa, wg, k_k, k_a, seq):
    t = p_rw.shape[0]
    ts = _tile(seq, 256, RWKV_CHUNK)
    assert ts % RWKV_CHUNK == 0
    hb = ts // SUBLANES
    w = RWKV_W
    row = lambda n: pl.BlockSpec((1, n), lambda i: (0, 0))
    full = lambda a: pl.BlockSpec(a.shape, lambda i: (0, 0))
    out = jax.ShapeDtypeStruct((t, w), F32)
    return pl.pallas_call(
        functools.partial(_rw_prep_kernel, tiles_per_seq=seq // ts, chunk=RWKV_CHUNK),
        grid=(t // ts,),
        in_specs=[pl.BlockSpec((ts, RWKV_COLS), lambda i: (i, 0)),
                  pl.BlockSpec((SUBLANES, RWKV_COLS), lambda i: (jnp.maximum(i * hb - 1, 0), 0)),
                  row(RWKV_COLS), row(w), full(wd), row(w), full(wa), full(wg), row(w), row(w)],
        out_specs=[pl.BlockSpec((ts, w), lambda i: (i, 0))] * 7,
        out_shape=[out] * 7,
        compiler_params=_params("parallel"),
        name="rwkv_prep",
    )(p_rw, p_rw, mu.reshape(1, -1), w0.reshape(1, w), wd, a0.reshape(1, w), wa, wg,
      k_k.reshape(1, w), k_a.reshape(1, w))


RW_PASSES = dict(m=1, inv=1, apply=1, gh=1, state=1, norm=1)


def _rw_chunk(rs, cums, ks, vs, a_tils, bs, zs, c, ps):
    n = 2 * RWKV_HEAD_DIM
    row = lax.broadcasted_iota(jnp.int32, (c, n), 0)
    lane = lax.broadcasted_iota(jnp.int32, (c, n), 1)
    lane_a = lane < RWKV_HEAD_DIM
    src = jnp.where(lane_a, lane, lane - RWKV_HEAD_DIM)
    strict = src < row
    incl = src <= row
    sq_r = lax.broadcasted_iota(jnp.int32, (n, n), 0)
    sq_c = lax.broadcasted_iota(jnp.int32, (n, n), 1)
    same_head = (sq_r < RWKV_HEAD_DIM) == (sq_c < RWKV_HEAD_DIM)
    eye = sq_r == sq_c
    each = lambda f, *lists: [f(*xs) for xs in zip(*lists)]
    cat = lambda *xs: jnp.concatenate(xs, axis=0)
    only_a = lambda x: jnp.where(lane_a, x, 0.0)
    only_b = lambda x: jnp.where(lane_a, 0.0, x)

    def fold(stacked):
        kept = jnp.where(same_head, stacked, 0.0)
        return kept[:c] + kept[c:]

    cum_ends = each(lambda cum: cum[c - 1:c, :], cums)
    grows = each(jnp.exp, cums)
    invs = each(lambda cum: jnp.exp(-cum), cums)
    to_ends = each(lambda cum, ce: jnp.exp(ce - cum), cums, cum_ends)
    a_hats = each(jnp.multiply, a_tils, grows)
    r_hats = each(jnp.multiply, rs, grows)
    b_chks = each(jnp.multiply, bs, invs)
    k_chks = each(jnp.multiply, ks, invs)
    b_ends = each(jnp.multiply, bs, to_ends)
    k_ends = each(jnp.multiply, ks, to_ends)

    m_as = each(lambda ah, rh, bc, kc: _pdot(cat(only_a(ah), only_a(rh)), cat(bc, kc), ps["m"], nt=True),
                a_hats, r_hats, b_chks, k_chks)
    m_bs = each(lambda ah, rh, bc, kc: _pdot(cat(only_b(ah), only_b(rh)), cat(kc, bc), ps["m"], nt=True),
                a_hats, r_hats, b_chks, k_chks)
    ma_tops = each(lambda m: jnp.where(strict, m[:c], 0.0), m_as)
    ma_bots = each(lambda m: jnp.where(incl, m[c:], 0.0), m_as)
    mb_tops = each(lambda m: jnp.where(strict, m[:c], 0.0), m_bs)
    mb_bots = each(lambda m: jnp.where(incl, m[c:], 0.0), m_bs)

    powers = each(lambda ta, tb: cat(only_a(ta), only_b(tb)), ma_tops, mb_tops)
    t_invs = each(lambda nil: jnp.where(eye, 1.0, 0.0) + nil, powers)
    for _ in range(c.bit_length() - 2):
        powers = each(lambda pw: _pdot(pw, pw, ps["inv"]), powers)
        t_invs = each(lambda ti, pw: ti + _pdot(ti, pw, ps["inv"]), t_invs, powers)

    a_news = each(lambda ti, ah: fold(_pdot(ti, cat(only_a(ah), only_b(ah)), ps["apply"])), t_invs, a_hats)
    w1s = each(lambda ta, tb, v: jnp.where(same_head, _pdot(cat(only_b(ta), only_a(tb)), cat(v, v), ps["apply"]), 0.0),
               ma_tops, mb_tops, vs)
    v_news = each(lambda ti, w1: fold(_pdot(ti, w1, ps["apply"])), t_invs, w1s)
    r_news = each(lambda rh, ba, bb, an: rh + fold(_pdot(cat(only_a(ba), only_b(bb)), cat(an, an), ps["apply"])),
                  r_hats, ma_bots, mb_bots, a_news)
    y_ins = each(lambda ba, bb, vn, v: jnp.where(lane_a, _pdot(ba, cat(vn, v), ps["apply"]),
                                                 _pdot(bb, cat(v, vn), ps["apply"])),
                 ma_bots, mb_bots, v_news, vs)
    g_mats = each(lambda be, an, ce: (jnp.where(same_head, _pdot(be.T, an, ps["gh"]), 0.0)
                                      + jnp.where(eye, jnp.broadcast_to(jnp.exp(ce), (n, n)), 0.0)),
                  b_ends, a_news, cum_ends)
    h_mats = each(lambda be, ke, vn, v: jnp.where(same_head, _pdot(cat(be, ke).T, cat(vn, v), ps["gh"]), 0.0),
                  b_ends, k_ends, v_news, vs)
    ys = each(lambda rn, z, yi: _pdot(rn, z, ps["state"]) + yi, r_news, zs, y_ins)
    z_news = each(lambda g, z, h: _pdot(g, z, ps["state"]) + h, g_mats, zs, h_mats)
    return ys, z_news


def _rw_scan_kernel(r_ref, cum_ref, k_ref, v_ref, a_ref, b_ref, g_ref, rk_ref, lnw_ref, lnb_ref,
                    o_ref, z_ref, *, chunk, pairs, passes):
    n = 2 * RWKV_HEAD_DIM
    assert n == LANES and 2 * chunk == n

    @pl.when(pl.program_id(2) == 0)
    def _():
        z_ref[...] = jnp.zeros_like(z_ref)

    sq_r = lax.broadcasted_iota(jnp.int32, (n, n), 0)
    sq_c = lax.broadcasted_iota(jnp.int32, (n, n), 1)
    same_head = (sq_r < RWKV_HEAD_DIM) == (sq_c < RWKV_HEAD_DIM)
    head_mean = jnp.where(same_head, 1.0 / RWKV_HEAD_DIM, 0.0).astype(F32)
    head_ones = jnp.where(same_head, 1.0, 0.0).astype(F32)
    sls = [slice(p * n, (p + 1) * n) for p in range(pairs)]
    take = lambda ref: [ref[:, sl] for sl in sls]
    rs, ks, vs = take(r_ref), take(k_ref), take(v_ref)
    ys, z_news = _rw_chunk(rs, take(cum_ref), ks, vs, take(a_ref), take(b_ref),
                           [z_ref[p] for p in range(pairs)], chunk, passes)
    for p in range(pairs):
        z_ref[p] = z_news[p]
    means = [_pdot(y, head_mean, passes["norm"]) for y in ys]
    devs = [y - m for y, m in zip(ys, means)]
    variances = [_pdot(d * d, head_mean, passes["norm"]) for d in devs]
    bonuses = [_pdot(r * k * rk_ref[:, sl], head_ones, passes["norm"]) * v for r, k, v, sl in zip(rs, ks, vs, sls)]
    for d, var, bonus, sl in zip(devs, variances, bonuses, sls):
        yn = d * lax.rsqrt(var + LNX_EPS) * lnw_ref[:, sl] + lnb_ref[:, sl]
        o_ref[:, sl] = ((yn + bonus) * g_ref[:, sl]).astype(o_ref.dtype)


def rwkv_scan(r, cum, k, v, a_til, b, g, r_k, lnx_w, lnx_b, batch, seq, pairs_per_step=8):
    t = r.shape[0]
    c = RWKV_CHUNK
    nc = seq // c
    n = 2 * RWKV_HEAD_DIM
    pp = pairs_per_step
    groups = RWKV_W // (n * pp)
    tok = pl.BlockSpec((c, n * pp), lambda bi, p, ci: (bi * nc + ci, p))
    par = pl.BlockSpec((1, n * pp), lambda bi, p, ci: (0, p))
    return pl.pallas_call(
        functools.partial(_rw_scan_kernel, chunk=c, pairs=pp, passes=dict(RW_PASSES)),
        grid=(batch, groups, nc),
        in_specs=[tok] * 7 + [par] * 3,
        out_specs=tok,
        out_shape=jax.ShapeDtypeStruct((t, RWKV_W), BF16),
        scratch_shapes=[pltpu.VMEM((pp, n, n), F32)],
        compiler_params=_params("parallel", "parallel", "arbitrary"),
        name="rwkv_scan",
    )(r, cum, k, v, a_til, b, g, r_k.reshape(1, RWKV_W), lnx_w.reshape(1, RWKV_W), lnx_b.reshape(1, RWKV_W))


def _merge_kernel(att_ref, conv_ref, rw_ref, gd_ref, wa_ref, wc_ref, wr_ref,
                  ga_ref, gc_ref, gr_ref, ba_ref, bc_ref, br_ref, o_ref):
    gd = gd_ref[...]

    def branch(x_ref, w_ref, gw_ref, gb_ref):
        y = jnp.dot(x_ref[...], w_ref[0], preferred_element_type=F32)
        gate = jax.nn.sigmoid(jnp.dot(gd, gw_ref[0], preferred_element_type=F32) + gb_ref[0])
        return gate * y

    out = branch(att_ref, wa_ref, ga_ref, ba_ref)
    out = out + branch(conv_ref, wc_ref, gc_ref, bc_ref)
    out = out + branch(rw_ref, wr_ref, gr_ref, br_ref)
    o_ref[...] = out.astype(o_ref.dtype)


def gated_merge(att, conv, rw, gd, w_att_o, w_conv_o, w_rwkv_o, w_gate_up, b_gate, layer):
    t = att.shape[0]
    d = w_att_o.shape[2]
    tm = _tile(t, 1024, SUBLANES)
    tn = _tile(d, 512, LANES)
    nj = d // tn
    act = lambda w: pl.BlockSpec((tm, w), lambda i, j: (i, 0))
    wout = lambda w: pl.BlockSpec((1, w, tn), lambda i, j: (layer, 0, j))
    gate_w = lambda br: pl.BlockSpec((1, MERGE_RANK, tn), lambda i, j: (layer, 0, br * nj + j))
    gate_b = lambda br: pl.BlockSpec((1, 1, tn), lambda i, j: (layer, 0, br * nj + j))
    bg = b_gate.reshape(b_gate.shape[0], 1, N_BRANCH * d)
    return pl.pallas_call(
        _merge_kernel,
        grid=(t // tm, nj),
        in_specs=[act(ATT_W), act(CONV_W), act(RWKV_W), act(MERGE_RANK),
                  wout(ATT_W), wout(CONV_W), wout(RWKV_W),
                  gate_w(0), gate_w(1), gate_w(2), gate_b(0), gate_b(1), gate_b(2)],
        out_specs=pl.BlockSpec((tm, tn), lambda i, j: (i, j)),
        out_shape=jax.ShapeDtypeStruct((t, d), BF16),
        compiler_params=_params("parallel", "parallel"),
        name="gated_merge",
    )(att, conv, rw, gd, w_att_o, w_conv_o, w_rwkv_o, w_gate_up, w_gate_up, w_gate_up, bg, bg, bg)


def _pad_rows(w, rows_before, total):
    return jnp.pad(w, ((rows_before, total - rows_before - w.shape[0]), (0, 0)))


def kernel(x, norm_mix, w_in, w_att_o, conv_w, w_conv_o, rwkv_mu, rwkv_w0, rwkv_w_decay_up, rwkv_a0, rwkv_w_a_up, rwkv_w_g_up, rwkv_k_k, rwkv_k_a, rwkv_r_k, rwkv_lnx_w, rwkv_lnx_b, w_rwkv_o, w_gate_up, b_gate, w_out, norm_mlp, w_mlp_up, w_mlp_down, norm_final):
    batch, seq, d = x.shape
    depth = w_in.shape[0]
    t = batch * seq
    xs = x.reshape(t, d)
    att_end = 3 * ATT_W
    conv_end = att_end + 3 * CONV_W
    rw_end = conv_end + 3 * RWKV_W + LORA_BLOCK + GATE_LORA
    gate_pad = GATE_LORA_PAD - GATE_LORA
    w_in_b = w_in[:, :, :conv_end].astype(BF16)
    w_rw_b = jnp.pad(w_in[:, :, conv_end:rw_end].astype(BF16), ((0, 0), (0, 0), (0, gate_pad)))
    w_gd_b = w_in[:, :, rw_end:].astype(BF16)
    w_att_o_b, w_conv_o_b, w_rwkv_o_b = w_att_o.astype(BF16), w_conv_o.astype(BF16), w_rwkv_o.astype(BF16)
    w_gate_up_b, w_out_b = w_gate_up.astype(BF16), w_out.astype(BF16)
    w_mlp_up_b, w_mlp_down_b = w_mlp_up.astype(BF16), w_mlp_down.astype(BF16)
    q_scale = jnp.concatenate([jnp.full((1, ATT_W), ATT_HEAD_DIM ** -0.5 * LOG2_E, F32), jnp.ones((1, 2 * ATT_W), F32)], axis=1)
    for l in range(depth):
        mu = jnp.pad(rwkv_mu[l], (0, gate_pad))
        wd = _pad_rows(rwkv_w_decay_up[l], 0, LORA_BLOCK)
        wa = _pad_rows(rwkv_w_a_up[l], DECAY_LORA, LORA_BLOCK)
        wg = _pad_rows(rwkv_w_g_up[l], 0, GATE_LORA_PAD)

        h = rmsnorm(xs, norm_mix[l], BF16)
        p_att = matmul(h, w_in_b, l, BF16, epilogue="colscale", extra=q_scale, ncols=att_end)
        p_conv = matmul(h, w_in_b, l, F32, col0=att_end, ncols=conv_end - att_end)
        p_rw = matmul(h, w_rw_b, l, F32, tn=1152)
        gd = matmul(h, w_gd_b, l, BF16)

        y_att = stick_breaking_attention(p_att, batch, seq)
        y_conv = short_gated_conv(p_conv, conv_w[l], seq)
        r, cum, k, v, a_til, b, g = rwkv_prep(p_rw, mu, rwkv_w0[l], wd, rwkv_a0[l], wa, wg,
                                             rwkv_k_k[l], rwkv_k_a[l], seq)
        y_rw = rwkv_scan(r, cum, k, v, a_til, b, g, rwkv_r_k[l], rwkv_lnx_w[l], rwkv_lnx_b[l], batch, seq)

        merged = gated_merge(y_att, y_conv, y_rw, gd, w_att_o_b, w_conv_o_b, w_rwkv_o_b, w_gate_up_b, b_gate, l)
        xs = matmul(merged, w_out_b, l, F32, epilogue="residual", extra=xs)

        h = rmsnorm(xs, norm_mlp[l], BF16)
        up = matmul(h, w_mlp_up_b, l, BF16, epilogue="relu2")
        xs = matmul(up, w_mlp_down_b, l, F32, epilogue="residual", extra=xs, tk=2048)
    return rmsnorm(xs, norm_final, F32).reshape(batch, seq, d)
```

```python
import functools

import jax
import jax.numpy as jnp
from jax import lax
from jax.experimental import pallas as pl
from jax.experimental.pallas import tpu as pltpu

F32 = jnp.float32
BF16 = jnp.bfloat16

LANES = 128
SUBLANES = 8
VMEM_LIMIT_BYTES = 56 * 1024 * 1024

ATT_HEADS = 8
ATT_HEAD_DIM = 128
ATT_W = ATT_HEADS * ATT_HEAD_DIM
CONV_W = 1024
CONV_K = 3
RWKV_HEADS = 16
RWKV_HEAD_DIM = 64
RWKV_W = RWKV_HEADS * RWKV_HEAD_DIM
DECAY_LORA = 64
AAA_LORA = 64
GATE_LORA = 160
GATE_LORA_PAD = 256
LORA_BLOCK = DECAY_LORA + AAA_LORA
RWKV_COLS = 3 * RWKV_W + LORA_BLOCK + GATE_LORA_PAD
MERGE_RANK = 256
N_BRANCH = 3
RMS_EPS = 1e-6
LNX_EPS = 64e-5
KK_EPS = 1e-12
RWKV_CHUNK = 64
ATT_QUERY_BLOCK = 1024
ATT_KEY_BLOCK = 1024
ATT_GROUP = 256
ATT_DIAG_BLOCK = 512
LOG2_E = 1.4426950408889634


def _tile(dim, target, mult):
    best = None
    t = mult
    while t <= min(dim, target):
        if dim % t == 0:
            best = t
        t += mult
    return best if best is not None else dim


def _params(*sem):
    return pltpu.CompilerParams(dimension_semantics=sem, vmem_limit_bytes=VMEM_LIMIT_BYTES)


def _rmsnorm_kernel(x_ref, g_ref, o_ref):
    x = x_ref[...]
    ms = jnp.mean(x * x, axis=-1, keepdims=True)
    o_ref[...] = (x * lax.rsqrt(ms + RMS_EPS) * g_ref[...]).astype(o_ref.dtype)


def rmsnorm(x, g, out_dtype):
    t, d = x.shape
    tm = _tile(t, 512, SUBLANES)
    return pl.pallas_call(
        _rmsnorm_kernel,
        grid=(t // tm,),
        in_specs=[pl.BlockSpec((tm, d), lambda i: (i, 0)),
                  pl.BlockSpec((1, d), lambda i: (0, 0))],
        out_specs=pl.BlockSpec((tm, d), lambda i: (i, 0)),
        out_shape=jax.ShapeDtypeStruct((t, d), out_dtype),
        compiler_params=_params("parallel"),
        name="rmsnorm",
    )(x, g.reshape(1, d))


def _cast_kernel(x_ref, o_ref):
    o_ref[...] = x_ref[...].astype(o_ref.dtype)


def cast_columns(w, col0, ncols, tn, tk):
    nl, kdim, _ = w.shape
    assert col0 % tn == 0 and ncols % tn == 0
    cb = col0 // tn
    tk = _tile(kdim, tk, SUBLANES)
    return pl.pallas_call(
        _cast_kernel,
        grid=(nl, kdim // tk, ncols // tn),
        in_specs=[pl.BlockSpec((1, tk, tn), lambda l, k, j: (l, k, cb + j))],
        out_specs=pl.BlockSpec((1, tk, tn), lambda l, k, j: (l, k, j)),
        out_shape=jax.ShapeDtypeStruct((nl, kdim, ncols), BF16),
        compiler_params=_params("parallel", "parallel", "parallel"),
        name="cast_columns",
    )(w)


def _mm_epilogue(acc, epilogue, extra_ref):
    if epilogue == "relu2":
        acc = jnp.square(jnp.maximum(acc, 0.0))
    elif epilogue == "residual":
        acc = acc + extra_ref[...]
    elif epilogue == "colscale":
        acc = acc * extra_ref[...]
    return acc


def _mm_kernel(*refs, epilogue, nk):
    a_ref, w_ref = refs[:2]
    extra_ref = refs[2] if epilogue in ("residual", "colscale") else None
    o_ref = refs[3] if extra_ref is not None else refs[2]
    if nk == 1:
        acc = jnp.dot(a_ref[...], w_ref[0], preferred_element_type=F32)
        o_ref[...] = _mm_epilogue(acc, epilogue, extra_ref).astype(o_ref.dtype)
        return
    k = pl.program_id(2)
    if epilogue == "residual" and o_ref.dtype == F32:
        @pl.when(k == 0)
        def _():
            o_ref[...] = extra_ref[...] + jnp.dot(a_ref[...], w_ref[0], preferred_element_type=F32)

        @pl.when(k > 0)
        def _():
            o_ref[...] += jnp.dot(a_ref[...], w_ref[0], preferred_element_type=F32)
        return
    acc_ref = refs[-1]

    @pl.when(k == 0)
    def _():
        acc_ref[...] = jnp.zeros_like(acc_ref)

    acc_ref[...] += jnp.dot(a_ref[...], w_ref[0], preferred_element_type=F32)

    @pl.when(k == nk - 1)
    def _():
        o_ref[...] = _mm_epilogue(acc_ref[...], epilogue, extra_ref).astype(o_ref.dtype)


def matmul(a, w, layer, out_dtype, epilogue="none", extra=None, col0=0, ncols=None, tm=1024, tn=1024, tk=4096):
    m, kdim = a.shape
    n = w.shape[2] - col0 if ncols is None else ncols
    tm = _tile(m, tm, SUBLANES)
    tn = _tile(n, tn, LANES)
    assert col0 % tn == 0
    cb = col0 // tn
    tk = _tile(kdim, tk, LANES)
    nk = kdim // tk
    in_specs = [pl.BlockSpec((tm, tk), lambda i, j, k: (i, k)),
                pl.BlockSpec((1, tk, tn), lambda i, j, k: (layer, k, cb + j))]
    args = [a, w]
    if epilogue == "residual":
        in_specs.append(pl.BlockSpec((tm, tn), lambda i, j, k: (i, j)))
        args.append(extra)
    elif epilogue == "colscale":
        in_specs.append(pl.BlockSpec((1, tn), lambda i, j, k: (0, j)))
        args.append(extra)
    accumulate_in_output = epilogue == "residual" and out_dtype == F32
    return pl.pallas_call(
        functools.partial(_mm_kernel, epilogue=epilogue, nk=nk),
        grid=(m // tm, n // tn, nk),
        in_specs=in_specs,
        out_specs=pl.BlockSpec((tm, tn), lambda i, j, k: (i, j)),
        out_shape=jax.ShapeDtypeStruct((m, n), out_dtype),
        scratch_shapes=[pltpu.VMEM((tm, tn), F32)] if nk > 1 and not accumulate_in_output else [],
        compiler_params=_params("parallel", "parallel", "arbitrary"),
        name="matmul_" + epilogue,
    )(*args)


def _softplus(z):
    return jnp.maximum(z, 0.0) + jnp.log(1.0 + jnp.exp(-jnp.abs(z)))


def _softplus_base2(z2):
    neg_abs = lax.bitcast_convert_type(lax.bitcast_convert_type(z2, jnp.uint32) | jnp.uint32(0x80000000), F32)
    return jnp.maximum(z2, 0.0) + jnp.log(1.0 + jnp.exp2(neg_abs)) * LOG2_E


def _att_kernel(q_ref, k_ref, v_ref, o_ref, acc_ref, carry_ref, *, tq, tk, grp, diag):
    qi = pl.program_id(2)
    r = lax.broadcasted_iota(jnp.int32, (grp, grp), 0)
    c = lax.broadcasted_iota(jnp.int32, (grp, grp), 1)
    later = jnp.where(r > c, 1.0, 0.0).astype(BF16)

    def visit(kstart, kw, r0, nr, masked):
        rows = slice(r0, r0 + nr)
        kb = k_ref[pl.ds(kstart, kw), :]
        vb = v_ref[pl.ds(kstart, kw), :]
        z = lax.dot_general(q_ref[rows, :], kb, (((1,), (1,)), ((), ())), preferred_element_type=F32)
        ng = kw // grp
        zs = [z[:, g * grp:(g + 1) * grp] for g in range(ng)]
        sps = [_softplus_base2(zg) for zg in zs]
        if masked:
            row = lax.broadcasted_iota(jnp.int32, (nr, grp), 0)
            col = lax.broadcasted_iota(jnp.int32, (nr, grp), 1)
            causal = [col + g * grp < row for g in range(ng)]
            spms = [jnp.where(cg, sp, 0.0) for cg, sp in zip(causal, sps)]
        else:
            spms = sps
        afters = [jnp.dot(sp.astype(BF16), later, preferred_element_type=F32) for sp in spms]
        totals = [af[:, 0:1] + sp[:, 0:1] for af, sp in zip(afters, spms)]
        carry = carry_ref[rows, :]
        ws = [None] * ng
        for g in reversed(range(ng)):
            w = jnp.exp2(zs[g] - sps[g] - afters[g] - carry)
            if masked:
                w = jnp.where(causal[g], w, 0.0)
            ws[g] = w.astype(BF16)
            carry = carry + totals[g]
        carry_ref[rows, :] = carry
        acc_ref[rows, :] += jnp.dot(jnp.concatenate(ws, axis=1), vb, preferred_element_type=F32)

    def own_span(r0, n):
        if n <= diag:
            visit(pl.multiple_of(qi * tq + r0, diag), n, r0, n, True)
            return
        half = n // 2
        own_span(r0 + half, half)
        visit(pl.multiple_of(qi * tq + r0, diag), half, r0 + half, half, False)
        own_span(r0, half)

    acc_ref[...] = jnp.zeros_like(acc_ref)
    carry_ref[...] = jnp.zeros_like(carry_ref)
    own_span(0, tq)

    def body(j, carry):
        visit(pl.multiple_of((qi * (tq // tk) - 1 - j) * tk, tk), tk, 0, tq, False)
        return carry

    lax.fori_loop(0, qi * (tq // tk), body, 0)
    o_ref[...] = acc_ref[...].astype(o_ref.dtype)


def stick_breaking_attention(p_att, batch, seq):
    tk = _tile(seq, ATT_KEY_BLOCK, LANES)
    tq = _tile(seq, ATT_QUERY_BLOCK, tk)
    nq = seq // tq
    d = ATT_HEAD_DIM
    return pl.pallas_call(
        functools.partial(_att_kernel, tq=tq, tk=tk, grp=ATT_GROUP, diag=_tile(tq, ATT_DIAG_BLOCK, ATT_GROUP)),
        grid=(batch, ATT_HEADS, nq),
        in_specs=[pl.BlockSpec((tq, d), lambda b, h, i: (b * nq + i, h)),
                  pl.BlockSpec((seq, d), lambda b, h, i: (b, ATT_HEADS + h)),
                  pl.BlockSpec((seq, d), lambda b, h, i: (b, 2 * ATT_HEADS + h))],
        out_specs=pl.BlockSpec((tq, d), lambda b, h, i: (b * nq + i, h)),
        out_shape=jax.ShapeDtypeStruct((batch * seq, ATT_W), BF16),
        scratch_shapes=[pltpu.VMEM((tq, d), F32), pltpu.VMEM((tq, 1), F32)],
        compiler_params=_params("parallel", "parallel", "arbitrary"),
        name="stick_breaking_attention",
    )(p_att, p_att, p_att)


def _shift_rows(x, halo, n):
    row = lax.broadcasted_iota(jnp.int32, x.shape, 0)
    out = pltpu.roll(x, n, 0)
    for r in range(n):
        out = jnp.where(row == r, halo[SUBLANES - n + r:SUBLANES - n + r + 1, :], out)
    return out


def _conv_kernel(p_ref, halo_ref, w_ref, o_ref, *, tiles_per_seq):
    i = pl.program_id(0)
    first = (i % tiles_per_seq) == 0
    cw = CONV_W
    x = p_ref[:, cw:2 * cw] * p_ref[:, 2 * cw:3 * cw]
    hx = halo_ref[:, cw:2 * cw] * halo_ref[:, 2 * cw:3 * cw]
    hx = jnp.where(first, 0.0, hx)
    x1 = _shift_rows(x, hx, 1)
    x2 = _shift_rows(x, hx, 2)
    y = w_ref[0:1, :] * x2 + w_ref[1:2, :] * x1 + w_ref[2:3, :] * x
    o_ref[...] = (p_ref[:, 0:cw] * y).astype(o_ref.dtype)


def short_gated_conv(p_conv, conv_w, seq):
    t = p_conv.shape[0]
    ts = _tile(seq, 512, SUBLANES)
    hb = ts // SUBLANES
    return pl.pallas_call(
        functools.partial(_conv_kernel, tiles_per_seq=seq // ts),
        grid=(t // ts,),
        in_specs=[pl.BlockSpec((ts, 3 * CONV_W), lambda i: (i, 0)),
                  pl.BlockSpec((SUBLANES, 3 * CONV_W), lambda i: (jnp.maximum(i * hb - 1, 0), 0)),
                  pl.BlockSpec((CONV_K, CONV_W), lambda i: (0, 0))],
        out_specs=pl.BlockSpec((ts, CONV_W), lambda i: (i, 0)),
        out_shape=jax.ShapeDtypeStruct((t, CONV_W), BF16),
        compiler_params=_params("parallel"),
        name="short_gated_conv",
    )(p_conv, p_conv, conv_w)


def _trunc_bf16(x):
    bits = lax.bitcast_convert_type(x, jnp.uint32) & jnp.uint32(0xFFFF0000)
    hi = lax.bitcast_convert_type(bits, F32)
    return hi.astype(BF16), x - hi


def _bf16_pieces(x, terms):
    pieces = []
    for _ in range(terms - 1):
        piece, x = _trunc_bf16(x)
        pieces.append(piece)
    return pieces + [x.astype(BF16)]


def _pdot(x, y, passes, nt=False):
    dims = (((1,), (1,)), ((), ())) if nt else (((1,), (0,)), ((), ()))
    dot = lambda p, q: lax.dot_general(p, q, dims, preferred_element_type=F32)
    if passes == 1:
        return dot(x.astype(BF16), y.astype(BF16))
    assert passes == 3
    xh, xl = _bf16_pieces(x, 2)
    yh, yl = _bf16_pieces(y, 2)
    return dot(xh, yh) + dot(xh, yl) + dot(xl, yh)


def _dot_with_mask(x, y, terms, mask_side):
    if mask_side == "lhs":
        mask = x.astype(BF16)
        parts = [jnp.dot(mask, p, preferred_element_type=F32) for p in _bf16_pieces(y, terms)]
    else:
        mask = y.astype(BF16)
        parts = [jnp.dot(p, mask, preferred_element_type=F32) for p in _bf16_pieces(x, terms)]
    return functools.reduce(jnp.add, parts)


def _head_sum_matrix(n, value):
    r = lax.broadcasted_iota(jnp.int32, (n, n), 0) // RWKV_HEAD_DIM
    c = lax.broadcasted_iota(jnp.int32, (n, n), 1) // RWKV_HEAD_DIM
    return jnp.where(r == c, value, 0.0).astype(F32)


def _rw_prep_kernel(p_ref, halo_ref, mu_ref, w0_ref, wd_ref, a0_ref, wa_ref, wg_ref, kk_ref, ka_ref,
                    r_ref, cum_ref, k_ref, v_ref, a_ref, b_ref, g_ref, *, tiles_per_seq, chunk):
    i = pl.program_id(0)
    first = (i % tiles_per_seq) == 0
    p = p_ref[...]
    halo = jnp.where(first, 0.0, halo_ref[...])
    prev = _shift_rows(p, halo, 1)
    seg = p + mu_ref[...] * (prev - p)
    w = RWKV_W
    r = seg[:, 0:w]
    k = seg[:, w:2 * w]
    v = seg[:, 2 * w:3 * w]
    lora = seg[:, 3 * w:3 * w + LORA_BLOCK]
    lg = seg[:, 3 * w + LORA_BLOCK:]
    wlog = w0_ref[...] + _pdot(jnp.tanh(lora), wd_ref[...], 3)
    wlog = -_softplus(-wlog) - 0.5
    lw = -jnp.exp(wlog)
    ts = p.shape[0]
    ti = lax.broadcasted_iota(jnp.int32, (ts, ts), 0)
    si = lax.broadcasted_iota(jnp.int32, (ts, ts), 1)
    in_chunk_prefix = jnp.where((si <= ti) & (si // chunk == ti // chunk), 1.0, 0.0)
    cum_ref[...] = _dot_with_mask(in_chunk_prefix, lw, 3, "lhs")
    ag = jax.nn.sigmoid(a0_ref[...] + _pdot(lora, wa_ref[...], 3))
    g_ref[...] = _pdot(jax.nn.sigmoid(lg), wg_ref[...], 3)
    kk = k * kk_ref[...]
    head_ones = _head_sum_matrix(LANES, 1.0)
    sq = kk * kk
    ss = jnp.concatenate(
        [_dot_with_mask(sq[:, c:c + LANES], head_ones, 2, "rhs") for c in range(0, w, LANES)], axis=1)
    kk = kk / jnp.maximum(jnp.sqrt(ss), KK_EPS)
    r_ref[...] = r
    k_ref[...] = k * (1.0 + (ag - 1.0) * ka_ref[...])
    v_ref[...] = v
    a_ref[...] = -kk * jnp.exp(-lw)
    b_ref[...] = kk * ag


def rwkv_prep(p_rw, mu, w0, wd, a0, wa, wg, k_k, k_a, seq):
    t = p_rw.shape[0]
    ts = _tile(seq, 256, RWKV_CHUNK)
    assert ts % RWKV_CHUNK == 0
    hb = ts // SUBLANES
    w = RWKV_W
    row = lambda n: pl.BlockSpec((1, n), lambda i: (0, 0))
    full = lambda a: pl.BlockSpec(a.shape, lambda i: (0, 0))
    out = jax.ShapeDtypeStruct((t, w), F32)
    return pl.pallas_call(
        functools.partial(_rw_prep_kernel, tiles_per_seq=seq // ts, chunk=RWKV_CHUNK),
        grid=(t // ts,),
        in_specs=[pl.BlockSpec((ts, RWKV_COLS), lambda i: (i, 0)),
                  pl.BlockSpec((SUBLANES, RWKV_COLS), lambda i: (jnp.maximum(i * hb - 1, 0), 0)),
                  row(RWKV_COLS), row(w), full(wd), row(w), full(wa), full(wg), row(w), row(w)],
        out_specs=[pl.BlockSpec((ts, w), lambda i: (i, 0))] * 7,
        out_shape=[out] * 7,
        compiler_params=_params("parallel"),
        name="rwkv_prep",
    )(p_rw, p_rw, mu.reshape(1, -1), w0.reshape(1, w), wd, a0.reshape(1, w), wa, wg,
      k_k.reshape(1, w), k_a.reshape(1, w))


RW_PASSES = dict(m=1, inv=1, apply=1, gh=1, state=1, norm=1)


def _rw_chunk(rs, cums, ks, vs, a_tils, bs, zs, c, ps):
    n = 2 * RWKV_HEAD_DIM
    row = lax.broadcasted_iota(jnp.int32, (c, n), 0)
    lane = lax.broadcasted_iota(jnp.int32, (c, n), 1)
    lane_a = lane < RWKV_HEAD_DIM
    src = jnp.where(lane_a, lane, lane - RWKV_HEAD_DIM)
    strict = src < row
    incl = src <= row
    sq_r = lax.broadcasted_iota(jnp.int32, (n, n), 0)
    sq_c = lax.broadcasted_iota(jnp.int32, (n, n), 1)
    same_head = (sq_r < RWKV_HEAD_DIM) == (sq_c < RWKV_HEAD_DIM)
    eye = sq_r == sq_c
    each = lambda f, *lists: [f(*xs) for xs in zip(*lists)]
    cat = lambda *xs: jnp.concatenate(xs, axis=0)
    only_a = lambda x: jnp.where(lane_a, x, 0.0)
    only_b = lambda x: jnp.where(lane_a, 0.0, x)

    def fold(stacked):
        kept = jnp.where(same_head, stacked, 0.0)
        return kept[:c] + kept[c:]

    cum_ends = each(lambda cum: cum[c - 1:c, :], cums)
    grows = each(jnp.exp, cums)
    invs = each(lambda cum: jnp.exp(-cum), cums)
    to_ends = each(lambda cum, ce: jnp.exp(ce - cum), cums, cum_ends)
    a_hats = each(jnp.multiply, a_tils, grows)
    r_hats = each(jnp.multiply, rs, grows)
    b_chks = each(jnp.multiply, bs, invs)
    k_chks = each(jnp.multiply, ks, invs)
    b_ends = each(jnp.multiply, bs, to_ends)
    k_ends = each(jnp.multiply, ks, to_ends)

    m_as = each(lambda ah, rh, bc, kc: _pdot(cat(only_a(ah), only_a(rh)), cat(bc, kc), ps["m"], nt=True),
                a_hats, r_hats, b_chks, k_chks)
    m_bs = each(lambda ah, rh, bc, kc: _pdot(cat(only_b(ah), only_b(rh)), cat(kc, bc), ps["m"], nt=True),
                a_hats, r_hats, b_chks, k_chks)
    ma_tops = each(lambda m: jnp.where(strict, m[:c], 0.0), m_as)
    ma_bots = each(lambda m: jnp.where(incl, m[c:], 0.0), m_as)
    mb_tops = each(lambda m: jnp.where(strict, m[:c], 0.0), m_bs)
    mb_bots = each(lambda m: jnp.where(incl, m[c:], 0.0), m_bs)

    powers = each(lambda ta, tb: cat(only_a(ta), only_b(tb)), ma_tops, mb_tops)
    t_invs = each(lambda nil: jnp.where(eye, 1.0, 0.0) + nil, powers)
    for _ in range(c.bit_length() - 2):
        powers = each(lambda pw: _pdot(pw, pw, ps["inv"]), powers)
        t_invs = each(lambda ti, pw: ti + _pdot(ti, pw, ps["inv"]), t_invs, powers)

    a_news = each(lambda ti, ah: fold(_pdot(ti, cat(only_a(ah), only_b(ah)), ps["apply"])), t_invs, a_hats)
    w1s = each(lambda ta, tb, v: jnp.where(same_head, _pdot(cat(only_b(ta), only_a(tb)), cat(v, v), ps["apply"]), 0.0),
               ma_tops, mb_tops, vs)
    v_news = each(lambda ti, w1: fold(_pdot(ti, w1, ps["apply"])), t_invs, w1s)
    r_news = each(lambda rh, ba, bb, an: rh + fold(_pdot(cat(only_a(ba), only_b(bb)), cat(an, an), ps["apply"])),
                  r_hats, ma_bots, mb_bots, a_news)
    y_ins = each(lambda ba, bb, vn, v: jnp.where(lane_a, _pdot(ba, cat(vn, v), ps["apply"]),
                                                 _pdot(bb, cat(v, vn), ps["apply"])),
                 ma_bots, mb_bots, v_news, vs)
    g_mats = each(lambda be, an, ce: (jnp.where(same_head, _pdot(be.T, an, ps["gh"]), 0.0)
                                      + jnp.where(eye, jnp.broadcast_to(jnp.exp(ce), (n, n)), 0.0)),
                  b_ends, a_news, cum_ends)
    h_mats = each(lambda be, ke, vn, v: jnp.where(same_head, _pdot(cat(be, ke).T, cat(vn, v), ps["gh"]), 0.0),
                  b_ends, k_ends, v_news, vs)
    ys = each(lambda rn, z, yi: _pdot(rn, z, ps["state"]) + yi, r_news, zs, y_ins)
    z_news = each(lambda g, z, h: _pdot(g, z, ps["state"]) + h, g_mats, zs, h_mats)
    return ys, z_news


def _rw_scan_kernel(r_ref, cum_ref, k_ref, v_ref, a_ref, b_ref, g_ref, rk_ref, lnw_ref, lnb_ref,
                    o_ref, z_ref, *, chunk, pairs, passes):
    n = 2 * RWKV_HEAD_DIM
    assert n == LANES and 2 * chunk == n

    @pl.when(pl.program_id(2) == 0)
    def _():
        z_ref[...] = jnp.zeros_like(z_ref)

    sq_r = lax.broadcasted_iota(jnp.int32, (n, n), 0)
    sq_c = lax.broadcasted_iota(jnp.int32, (n, n), 1)
    same_head = (sq_r < RWKV_HEAD_DIM) == (sq_c < RWKV_HEAD_DIM)
    head_mean = jnp.where(same_head, 1.0 / RWKV_HEAD_DIM, 0.0).astype(F32)
    head_ones = jnp.where(same_head, 1.0, 0.0).astype(F32)
    sls = [slice(p * n, (p + 1) * n) for p in range(pairs)]
    take = lambda ref: [ref[:, sl] for sl in sls]
    rs, ks, vs = take(r_ref), take(k_ref), take(v_ref)
    ys, z_news = _rw_chunk(rs, take(cum_ref), ks, vs, take(a_ref), take(b_ref),
                           [z_ref[p] for p in range(pairs)], chunk, passes)
    for p in range(pairs):
        z_ref[p] = z_news[p]
    means = [_pdot(y, head_mean, passes["norm"]) for y in ys]
    devs = [y - m for y, m in zip(ys, means)]
    variances = [_pdot(d * d, head_mean, passes["norm"]) for d in devs]
    bonuses = [_pdot(r * k * rk_ref[:, sl], head_ones, passes["norm"]) * v for r, k, v, sl in zip(rs, ks, vs, sls)]
    for d, var, bonus, sl in zip(devs, variances, bonuses, sls):
        yn = d * lax.rsqrt(var + LNX_EPS) * lnw_ref[:, sl] + lnb_ref[:, sl]
        o_ref[:, sl] = ((yn + bonus) * g_ref[:, sl]).astype(o_ref.dtype)


def rwkv_scan(r, cum, k, v, a_til, b, g, r_k, lnx_w, lnx_b, batch, seq, pairs_per_step=8):
    t = r.shape[0]
    c = RWKV_CHUNK
    nc = seq // c
    n = 2 * RWKV_HEAD_DIM
    pp = pairs_per_step
    groups = RWKV_W // (n * pp)
    tok = pl.BlockSpec((c, n * pp), lambda bi, p, ci: (bi * nc + ci, p))
    par = pl.BlockSpec((1, n * pp), lambda bi, p, ci: (0, p))
    return pl.pallas_call(
        functools.partial(_rw_scan_kernel, chunk=c, pairs=pp, passes=dict(RW_PASSES)),
        grid=(batch, groups, nc),
        in_specs=[tok] * 7 + [par] * 3,
        out_specs=tok,
        out_shape=jax.ShapeDtypeStruct((t, RWKV_W), BF16),
        scratch_shapes=[pltpu.VMEM((pp, n, n), F32)],
        compiler_params=_params("parallel", "parallel", "arbitrary"),
        name="rwkv_scan",
    )(r, cum, k, v, a_til, b, g, r_k.reshape(1, RWKV_W), lnx_w.reshape(1, RWKV_W), lnx_b.reshape(1, RWKV_W))


def _merge_kernel(att_ref, conv_ref, rw_ref, gd_ref, wa_ref, wc_ref, wr_ref,
                  ga_ref, gc_ref, gr_ref, ba_ref, bc_ref, br_ref, o_ref):
    gd = gd_ref[...]

    def branch(x_ref, w_ref, gw_ref, gb_ref):
        y = jnp.dot(x_ref[...], w_ref[0], preferred_element_type=F32)
        gate = jax.nn.sigmoid(jnp.dot(gd, gw_ref[0], preferred_element_type=F32) + gb_ref[0])
        return gate * y

    out = branch(att_ref, wa_ref, ga_ref, ba_ref)
    out = out + branch(conv_ref, wc_ref, gc_ref, bc_ref)
    out = out + branch(rw_ref, wr_ref, gr_ref, br_ref)
    o_ref[...] = out.astype(o_ref.dtype)


def gated_merge(att, conv, rw, gd, w_att_o, w_conv_o, w_rwkv_o, w_gate_up, b_gate, layer):
    t = att.shape[0]
    d = w_att_o.shape[2]
    tm = _tile(t, 1024, SUBLANES)
    tn = _tile(d, 512, LANES)
    nj = d // tn
    act = lambda w: pl.BlockSpec((tm, w), lambda i, j: (i, 0))
    wout = lambda w: pl.BlockSpec((1, w, tn), lambda i, j: (layer, 0, j))
    gate_w = lambda br: pl.BlockSpec((1, MERGE_RANK, tn), lambda i, j: (layer, 0, br * nj + j))
    gate_b = lambda br: pl.BlockSpec((1, 1, tn), lambda i, j: (layer, 0, br * nj + j))
    bg = b_gate.reshape(b_gate.shape[0], 1, N_BRANCH * d)
    return pl.pallas_call(
        _merge_kernel,
        grid=(t // tm, nj),
        in_specs=[act(ATT_W), act(CONV_W), act(RWKV_W), act(MERGE_RANK),
                  wout(ATT_W), wout(CONV_W), wout(RWKV_W),
                  gate_w(0), gate_w(1), gate_w(2), gate_b(0), gate_b(1), gate_b(2)],
        out_specs=pl.BlockSpec((tm, tn), lambda i, j: (i, j)),
        out_shape=jax.ShapeDtypeStruct((t, d), BF16),
        compiler_params=_params("parallel", "parallel"),
        name="gated_merge",
    )(att, conv, rw, gd, w_att_o, w_conv_o, w_rwkv_o, w_gate_up, w_gate_up, w_gate_up, bg, bg, bg)


def _pad_rows(w, rows_before, total):
    return jnp.pad(w, ((rows_before, total - rows_before - w.shape[0]), (0, 0)))


def kernel(x, norm_mix, w_in, w_att_o, conv_w, w_conv_o, rwkv_mu, rwkv_w0, rwkv_w_decay_up, rwkv_a0, rwkv_w_a_up, rwkv_w_g_up, rwkv_k_k, rwkv_k_a, rwkv_r_k, rwkv_lnx_w, rwkv_lnx_b, w_rwkv_o, w_gate_up, b_gate, w_out, norm_mlp, w_mlp_up, w_mlp_down, norm_final):
    batch, seq, d = x.shape
    depth = w_in.shape[0]
    t = batch * seq
    xs = x.reshape(t, d)
    att_end = 3 * ATT_W
    conv_end = att_end + 3 * CONV_W
    rw_end = conv_end + 3 * RWKV_W + LORA_BLOCK + GATE_LORA
    gate_pad = GATE_LORA_PAD - GATE_LORA
    w_in_b = cast_columns(w_in, 0, conv_end, 1024, 1024)
    w_rw_b = cast_columns(w_in, conv_end, RWKV_COLS, 384, 2048)
    w_gd_b = w_in[:, :, rw_end:].astype(BF16)
    w_att_o_b, w_conv_o_b, w_rwkv_o_b = w_att_o.astype(BF16), w_conv_o.astype(BF16), w_rwkv_o.astype(BF16)
    w_gate_up_b, w_out_b = w_gate_up.astype(BF16), w_out.astype(BF16)
    w_mlp_up_b, w_mlp_down_b = w_mlp_up.astype(BF16), w_mlp_down.astype(BF16)
    q_scale = jnp.concatenate([jnp.full((1, ATT_W), ATT_HEAD_DIM ** -0.5 * LOG2_E, F32), jnp.ones((1, 2 * ATT_W), F32)], axis=1)
    for l in range(depth):
        mu = jnp.pad(rwkv_mu[l], (0, gate_pad))
        wd = _pad_rows(rwkv_w_decay_up[l], 0, LORA_BLOCK)
        wa = _pad_rows(rwkv_w_a_up[l], DECAY_LORA, LORA_BLOCK)
        wg = _pad_rows(rwkv_w_g_up[l], 0, GATE_LORA_PAD)

        h = rmsnorm(xs, norm_mix[l], BF16)
        p_att = matmul(h, w_in_b, l, BF16, epilogue="colscale", extra=q_scale, ncols=att_end)
        p_conv = matmul(h, w_in_b, l, F32, col0=att_end, ncols=conv_end - att_end)
        p_rw = matmul(h, w_rw_b, l, F32, tn=1152)
        gd = matmul(h, w_gd_b, l, BF16)

        y_att = stick_breaking_attention(p_att, batch, seq)
        y_conv = short_gated_conv(p_conv, conv_w[l], seq)
        r, cum, k, v, a_til, b, g = rwkv_prep(p_rw, mu, rwkv_w0[l], wd, rwkv_a0[l], wa, wg,
                                             rwkv_k_k[l], rwkv_k_a[l], seq)
        y_rw = rwkv_scan(r, cum, k, v, a_til, b, g, rwkv_r_k[l], rwkv_lnx_w[l], rwkv_lnx_b[l], batch, seq)

        merged = gated_merge(y_att, y_conv, y_rw, gd, w_att_o_b, w_conv_o_b, w_rwkv_o_b, w_gate_up_b, b_gate, l)
        xs = matmul(merged, w_out_b, l, F32, epilogue="residual", extra=xs)

        h = rmsnorm(xs, norm_mlp[l], BF16)
        up = matmul(h, w_mlp_up_b, l, BF16, epilogue="relu2")
        xs = matmul(up, w_mlp_down_b, l, F32, epilogue="residual", extra=xs, tk=2048)
    return rmsnorm(xs, norm_final, F32).reshape(batch, seq, d)
```

```python
import functools

import jax
import jax.numpy as jnp
from jax import lax
from jax.experimental import pallas as pl
from jax.experimental.pallas import tpu as pltpu

F32 = jnp.float32
BF16 = jnp.bfloat16

LANES = 128
SUBLANES = 8
VMEM_LIMIT_BYTES = 56 * 1024 * 1024

ATT_HEADS = 8
ATT_HEAD_DIM = 128
ATT_W = ATT_HEADS * ATT_HEAD_DIM
CONV_W = 1024
CONV_K = 3
RWKV_HEADS = 16
RWKV_HEAD_DIM = 64
RWKV_W = RWKV_HEADS * RWKV_HEAD_DIM
DECAY_LORA = 64
AAA_LORA = 64
GATE_LORA = 160
GATE_LORA_PAD = 256
LORA_BLOCK = DECAY_LORA + AAA_LORA
RWKV_COLS = 3 * RWKV_W + LORA_BLOCK + GATE_LORA_PAD
MERGE_RANK = 256
N_BRANCH = 3
RMS_EPS = 1e-6
LNX_EPS = 64e-5
KK_EPS = 1e-12
RWKV_CHUNK = 64
ATT_QUERY_BLOCK = 1024
ATT_KEY_BLOCK = 1024
ATT_GROUP = 256
ATT_DIAG_BLOCK = 512
LOG2_E = 1.4426950408889634


def _tile(dim, target, mult):
    best = None
    t = mult
    while t <= min(dim, target):
        if dim % t == 0:
            best = t
        t += mult
    return best if best is not None else dim


def _params(*sem):
    return pltpu.CompilerParams(dimension_semantics=sem, vmem_limit_bytes=VMEM_LIMIT_BYTES)


def _rmsnorm_kernel(x_ref, g_ref, o_ref):
    x = x_ref[...]
    ms = jnp.mean(x * x, axis=-1, keepdims=True)
    o_ref[...] = (x * lax.rsqrt(ms + RMS_EPS) * g_ref[...]).astype(o_ref.dtype)


def rmsnorm(x, g, out_dtype):
    t, d = x.shape
    tm = _tile(t, 512, SUBLANES)
    return pl.pallas_call(
        _rmsnorm_kernel,
        grid=(t // tm,),
        in_specs=[pl.BlockSpec((tm, d), lambda i: (i, 0)),
                  pl.BlockSpec((1, d), lambda i: (0, 0))],
        out_specs=pl.BlockSpec((tm, d), lambda i: (i, 0)),
        out_shape=jax.ShapeDtypeStruct((t, d), out_dtype),
        compiler_params=_params("parallel"),
        name="rmsnorm",
    )(x, g.reshape(1, d))


def _mm_epilogue(acc, epilogue, extra_ref):
    if epilogue == "relu2":
        acc = jnp.square(jnp.maximum(acc, 0.0))
    elif epilogue == "residual":
        acc = acc + extra_ref[...]
    elif epilogue == "colscale":
        acc = acc * extra_ref[...]
    return acc


def _mm_kernel(*refs, epilogue, nk):
    a_ref, w_ref = refs[:2]
    extra_ref = refs[2] if epilogue in ("residual", "colscale") else None
    o_ref = refs[3] if extra_ref is not None else refs[2]
    if nk == 1:
        acc = jnp.dot(a_ref[...], w_ref[0], preferred_element_type=F32)
        o_ref[...] = _mm_epilogue(acc, epilogue, extra_ref).astype(o_ref.dtype)
        return
    k = pl.program_id(2)
    if epilogue == "residual" and o_ref.dtype == F32:
        @pl.when(k == 0)
        def _():
            o_ref[...] = extra_ref[...] + jnp.dot(a_ref[...], w_ref[0], preferred_element_type=F32)

        @pl.when(k > 0)
        def _():
            o_ref[...] += jnp.dot(a_ref[...], w_ref[0], preferred_element_type=F32)
        return
    acc_ref = refs[-1]

    @pl.when(k == 0)
    def _():
        acc_ref[...] = jnp.zeros_like(acc_ref)

    acc_ref[...] += jnp.dot(a_ref[...], w_ref[0], preferred_element_type=F32)

    @pl.when(k == nk - 1)
    def _():
        o_ref[...] = _mm_epilogue(acc_ref[...], epilogue, extra_ref).astype(o_ref.dtype)


def matmul(a, w, layer, out_dtype, epilogue="none", extra=None, col0=0, ncols=None, tm=1024, tn=1024, tk=4096):
    m, kdim = a.shape
    n = w.shape[2] - col0 if ncols is None else ncols
    tm = _tile(m, tm, SUBLANES)
    tn = _tile(n, tn, LANES)
    assert col0 % tn == 0
    cb = col0 // tn
    tk = _tile(kdim, tk, LANES)
    nk = kdim // tk
    in_specs = [pl.BlockSpec((tm, tk), lambda i, j, k: (i, k)),
                pl.BlockSpec((1, tk, tn), lambda i, j, k: (layer, k, cb + j))]
    args = [a, w]
    if epilogue == "residual":
        in_specs.append(pl.BlockSpec((tm, tn), lambda i, j, k: (i, j)))
        args.append(extra)
    elif epilogue == "colscale":
        in_specs.append(pl.BlockSpec((1, tn), lambda i, j, k: (0, j)))
        args.append(extra)
    accumulate_in_output = epilogue == "residual" and out_dtype == F32
    return pl.pallas_call(
        functools.partial(_mm_kernel, epilogue=epilogue, nk=nk),
        grid=(m // tm, n // tn, nk),
        in_specs=in_specs,
        out_specs=pl.BlockSpec((tm, tn), lambda i, j, k: (i, j)),
        out_shape=jax.ShapeDtypeStruct((m, n), out_dtype),
        scratch_shapes=[pltpu.VMEM((tm, tn), F32)] if nk > 1 and not accumulate_in_output else [],
        compiler_params=_params("parallel", "parallel", "arbitrary"),
        name="matmul_" + epilogue,
    )(*args)


def _softplus(z):
    return jnp.maximum(z, 0.0) + jnp.log(1.0 + jnp.exp(-jnp.abs(z)))


def _softplus_base2(z2):
    neg_abs = lax.bitcast_convert_type(lax.bitcast_convert_type(z2, jnp.uint32) | jnp.uint32(0x80000000), F32)
    return jnp.maximum(z2, 0.0) + jnp.log(1.0 + jnp.exp2(neg_abs)) * LOG2_E


def _att_kernel(q_ref, k_ref, v_ref, o_ref, acc_ref, carry_ref, *, tq, tk, grp, diag):
    qi = pl.program_id(2)
    r = lax.broadcasted_iota(jnp.int32, (grp, grp), 0)
    c = lax.broadcasted_iota(jnp.int32, (grp, grp), 1)
    later = jnp.where(r > c, 1.0, 0.0).astype(BF16)

    def visit(kstart, kw, r0, nr, masked):
        rows = slice(r0, r0 + nr)
        kb = k_ref[pl.ds(kstart, kw), :]
        vb = v_ref[pl.ds(kstart, kw), :]
        z = lax.dot_general(q_ref[rows, :], kb, (((1,), (1,)), ((), ())), preferred_element_type=F32)
        ng = kw // grp
        zs = [z[:, g * grp:(g + 1) * grp] for g in range(ng)]
        sps = [_softplus_base2(zg) for zg in zs]
        if masked:
            row = lax.broadcasted_iota(jnp.int32, (nr, grp), 0)
            col = lax.broadcasted_iota(jnp.int32, (nr, grp), 1)
            causal = [col + g * grp < row for g in range(ng)]
            spms = [jnp.where(cg, sp, 0.0) for cg, sp in zip(causal, sps)]
        else:
            spms = sps
        afters = [jnp.dot(sp.astype(BF16), later, preferred_element_type=F32) for sp in spms]
        totals = [af[:, 0:1] + sp[:, 0:1] for af, sp in zip(afters, spms)]
        carry = carry_ref[rows, :]
        ws = [None] * ng
        for g in reversed(range(ng)):
            w = jnp.exp2(zs[g] - sps[g] - afters[g] - carry)
            if masked:
                w = jnp.where(causal[g], w, 0.0)
            ws[g] = w.astype(BF16)
            carry = carry + totals[g]
        carry_ref[rows, :] = carry
        acc_ref[rows, :] += jnp.dot(jnp.concatenate(ws, axis=1), vb, preferred_element_type=F32)

    def own_span(r0, n):
        if n <= diag:
            visit(pl.multiple_of(qi * tq + r0, diag), n, r0, n, True)
            return
        half = n // 2
        own_span(r0 + half, half)
        visit(pl.multiple_of(qi * tq + r0, diag), half, r0 + half, half, False)
        own_span(r0, half)

    acc_ref[...] = jnp.zeros_like(acc_ref)
    carry_ref[...] = jnp.zeros_like(carry_ref)
    own_span(0, tq)

    def body(j, carry):
        visit(pl.multiple_of((qi * (tq // tk) - 1 - j) * tk, tk), tk, 0, tq, False)
        return carry

    lax.fori_loop(0, qi * (tq // tk), body, 0)
    o_ref[...] = acc_ref[...].astype(o_ref.dtype)


def stick_breaking_attention(p_att, batch, seq):
    tk = _tile(seq, ATT_KEY_BLOCK, LANES)
    tq = _tile(seq, ATT_QUERY_BLOCK, tk)
    nq = seq // tq
    d = ATT_HEAD_DIM
    return pl.pallas_call(
        functools.partial(_att_kernel, tq=tq, tk=tk, grp=ATT_GROUP, diag=_tile(tq, ATT_DIAG_BLOCK, ATT_GROUP)),
        grid=(batch, ATT_HEADS, nq),
        in_specs=[pl.BlockSpec((tq, d), lambda b, h, i: (b * nq + i, h)),
                  pl.BlockSpec((seq, d), lambda b, h, i: (b, ATT_HEADS + h)),
                  pl.BlockSpec((seq, d), lambda b, h, i: (b, 2 * ATT_HEADS + h))],
        out_specs=pl.BlockSpec((tq, d), lambda b, h, i: (b * nq + i, h)),
        out_shape=jax.ShapeDtypeStruct((batch * seq, ATT_W), BF16),
        scratch_shapes=[pltpu.VMEM((tq, d), F32), pltpu.VMEM((tq, 1), F32)],
        compiler_params=_params("parallel", "parallel", "arbitrary"),
        name="stick_breaking_attention",
    )(p_att, p_att, p_att)


def _shift_rows(x, halo, n):
    row = lax.broadcasted_iota(jnp.int32, x.shape, 0)
    out = pltpu.roll(x, n, 0)
    for r in range(n):
        out = jnp.where(row == r, halo[SUBLANES - n + r:SUBLANES - n + r + 1, :], out)
    return out


def _conv_kernel(p_ref, halo_ref, w_ref, o_ref, *, tiles_per_seq):
    i = pl.program_id(0)
    first = (i % tiles_per_seq) == 0
    cw = CONV_W
    x = p_ref[:, cw:2 * cw] * p_ref[:, 2 * cw:3 * cw]
    hx = halo_ref[:, cw:2 * cw] * halo_ref[:, 2 * cw:3 * cw]
    hx = jnp.where(first, 0.0, hx)
    x1 = _shift_rows(x, hx, 1)
    x2 = _shift_rows(x, hx, 2)
    y = w_ref[0:1, :] * x2 + w_ref[1:2, :] * x1 + w_ref[2:3, :] * x
    o_ref[...] = (p_ref[:, 0:cw] * y).astype(o_ref.dtype)


def short_gated_conv(p_conv, conv_w, seq):
    t = p_conv.shape[0]
    ts = _tile(seq, 512, SUBLANES)
    hb = ts // SUBLANES
    return pl.pallas_call(
        functools.partial(_conv_kernel, tiles_per_seq=seq // ts),
        grid=(t // ts,),
        in_specs=[pl.BlockSpec((ts, 3 * CONV_W), lambda i: (i, 0)),
                  pl.BlockSpec((SUBLANES, 3 * CONV_W), lambda i: (jnp.maximum(i * hb - 1, 0), 0)),
                  pl.BlockSpec((CONV_K, CONV_W), lambda i: (0, 0))],
        out_specs=pl.BlockSpec((ts, CONV_W), lambda i: (i, 0)),
        out_shape=jax.ShapeDtypeStruct((t, CONV_W), BF16),
        compiler_params=_params("parallel"),
        name="short_gated_conv",
    )(p_conv, p_conv, conv_w)


def _trunc_bf16(x):
    bits = lax.bitcast_convert_type(x, jnp.uint32) & jnp.uint32(0xFFFF0000)
    hi = lax.bitcast_convert_type(bits, F32)
    return hi.astype(BF16), x - hi


def _bf16_pieces(x, terms):
    pieces = []
    for _ in range(terms - 1):
        piece, x = _trunc_bf16(x)
        pieces.append(piece)
    return pieces + [x.astype(BF16)]


def _pdot(x, y, passes, nt=False):
    dims = (((1,), (1,)), ((), ())) if nt else (((1,), (0,)), ((), ()))
    dot = lambda p, q: lax.dot_general(p, q, dims, preferred_element_type=F32)
    if passes == 1:
        return dot(x.astype(BF16), y.astype(BF16))
    assert passes == 3
    xh, xl = _bf16_pieces(x, 2)
    yh, yl = _bf16_pieces(y, 2)
    return dot(xh, yh) + dot(xh, yl) + dot(xl, yh)


def _dot_with_mask(x, y, terms, mask_side):
    if mask_side == "lhs":
        mask = x.astype(BF16)
        parts = [jnp.dot(mask, p, preferred_element_type=F32) for p in _bf16_pieces(y, terms)]
    else:
        mask = y.astype(BF16)
        parts = [jnp.dot(p, mask, preferred_element_type=F32) for p in _bf16_pieces(x, terms)]
    return functools.reduce(jnp.add, parts)


def _head_sum_matrix(n, value):
    r = lax.broadcasted_iota(jnp.int32, (n, n), 0) // RWKV_HEAD_DIM
    c = lax.broadcasted_iota(jnp.int32, (n, n), 1) // RWKV_HEAD_DIM
    return jnp.where(r == c, value, 0.0).astype(F32)


def _rw_prep_kernel(p_ref, halo_ref, mu_ref, w0_ref, wd_ref, a0_ref, wa_ref, wg_ref, kk_ref, ka_ref,
                    r_ref, cum_ref, k_ref, v_ref, a_ref, b_ref, g_ref, *, tiles_per_seq, chunk):
    i = pl.program_id(0)
    first = (i % tiles_per_seq) == 0
    p = p_ref[...]
    halo = jnp.where(first, 0.0, halo_ref[...])
    prev = _shift_rows(p, halo, 1)
    seg = p + mu_ref[...] * (prev - p)
    w = RWKV_W
    r = seg[:, 0:w]
    k = seg[:, w:2 * w]
    v = seg[:, 2 * w:3 * w]
    lora = seg[:, 3 * w:3 * w + LORA_BLOCK]
    lg = seg[:, 3 * w + LORA_BLOCK:]
    wlog = w0_ref[...] + _pdot(jnp.tanh(lora), wd_ref[...], 3)
    wlog = -_softplus(-wlog) - 0.5
    lw = -jnp.exp(wlog)
    ts = p.shape[0]
    ti = lax.broadcasted_iota(jnp.int32, (ts, ts), 0)
    si = lax.broadcasted_iota(jnp.int32, (ts, ts), 1)
    in_chunk_prefix = jnp.where((si <= ti) & (si // chunk == ti // chunk), 1.0, 0.0)
    cum_ref[...] = _dot_with_mask(in_chunk_prefix, lw, 3, "lhs")
    ag = jax.nn.sigmoid(a0_ref[...] + _pdot(lora, wa_ref[...], 3))
    g_ref[...] = _pdot(jax.nn.sigmoid(lg), wg_ref[...], 3)
    kk = k * kk_ref[...]
    head_ones = _head_sum_matrix(LANES, 1.0)
    sq = kk * kk
    ss = jnp.concatenate(
        [_dot_with_mask(sq[:, c:c + LANES], head_ones, 2, "rhs") for c in range(0, w, LANES)], axis=1)
    kk = kk / jnp.maximum(jnp.sqrt(ss), KK_EPS)
    r_ref[...] = r
    k_ref[...] = k * (1.0 + (ag - 1.0) * ka_ref[...])
    v_ref[...] = v
    a_ref[...] = -kk * jnp.exp(-lw)
    b_ref[...] = kk * ag


def rwkv_prep(p_rw, mu, w0, wd, a0, wa, wg, k_k, k_a, seq):
    t = p_rw.shape[0]
    ts = _tile(seq, 256, RWKV_CHUNK)
    assert ts % RWKV_CHUNK == 0
    hb = ts // SUBLANES
    w = RWKV_W
    row = lambda n: pl.BlockSpec((1, n), lambda i: (0, 0))
    full = lambda a: pl.BlockSpec(a.shape, lambda i: (0, 0))
    out = jax.ShapeDtypeStruct((t, w), F32)
    return pl.pallas_call(
        functools.partial(_rw_prep_kernel, tiles_per_seq=seq // ts, chunk=RWKV_CHUNK),
        grid=(t // ts,),
        in_specs=[pl.BlockSpec((ts, RWKV_COLS), lambda i: (i, 0)),
                  pl.BlockSpec((SUBLANES, RWKV_COLS), lambda i: (jnp.maximum(i * hb - 1, 0), 0)),
                  row(RWKV_COLS), row(w), full(wd), row(w), full(wa), full(wg), row(w), row(w)],
        out_specs=[pl.BlockSpec((ts, w), lambda i: (i, 0))] * 7,
        out_shape=[out] * 7,
        compiler_params=_params("parallel"),
        name="rwkv_prep",
    )(p_rw, p_rw, mu.reshape(1, -1), w0.reshape(1, w), wd, a0.reshape(1, w), wa, wg,
      k_k.reshape(1, w), k_a.reshape(1, w))


RW_PASSES = dict(m=1, inv=1, apply=1, gh=1, state=1, norm=1)


def _rw_chunk(rs, cums, ks, vs, a_tils, bs, zs, c, ps):
    n = 2 * RWKV_HEAD_DIM
    row = lax.broadcasted_iota(jnp.int32, (c, n), 0)
    lane = lax.broadcasted_iota(jnp.int32, (c, n), 1)
    lane_a = lane < RWKV_HEAD_DIM
    src = jnp.where(lane_a, lane, lane - RWKV_HEAD_DIM)
    strict = src < row
    incl = src <= row
    sq_r = lax.broadcasted_iota(jnp.int32, (n, n), 0)
    sq_c = lax.broadcasted_iota(jnp.int32, (n, n), 1)
    same_head = (sq_r < RWKV_HEAD_DIM) == (sq_c < RWKV_HEAD_DIM)
    eye = sq_r == sq_c
    each = lambda f, *lists: [f(*xs) for xs in zip(*lists)]
    cat = lambda *xs: jnp.concatenate(xs, axis=0)
    only_a = lambda x: jnp.where(lane_a, x, 0.0)
    only_b = lambda x: jnp.where(lane_a, 0.0, x)

    def fold(stacked):
        kept = jnp.where(same_head, stacked, 0.0)
        return kept[:c] + kept[c:]

    cum_ends = each(lambda cum: cum[c - 1:c, :], cums)
    grows = each(jnp.exp, cums)
    invs = each(lambda cum: jnp.exp(-cum), cums)
    to_ends = each(lambda cum, ce: jnp.exp(ce - cum), cums, cum_ends)
    a_hats = each(jnp.multiply, a_tils, grows)
    r_hats = each(jnp.multiply, rs, grows)
    b_chks = each(jnp.multiply, bs, invs)
    k_chks = each(jnp.multiply, ks, invs)
    b_ends = each(jnp.multiply, bs, to_ends)
    k_ends = each(jnp.multiply, ks, to_ends)

    m_as = each(lambda ah, rh, bc, kc: _pdot(cat(only_a(ah), only_a(rh)), cat(bc, kc), ps["m"], nt=True),
                a_hats, r_hats, b_chks, k_chks)
    m_bs = each(lambda ah, rh, bc, kc: _pdot(cat(only_b(ah), only_b(rh)), cat(kc, bc), ps["m"], nt=True),
                a_hats, r_hats, b_chks, k_chks)
    ma_tops = each(lambda m: jnp.where(strict, m[:c], 0.0), m_as)
    ma_bots = each(lambda m: jnp.where(incl, m[c:], 0.0), m_as)
    mb_tops = each(lambda m: jnp.where(strict, m[:c], 0.0), m_bs)
    mb_bots = each(lambda m: jnp.where(incl, m[c:], 0.0), m_bs)

    powers = each(lambda ta, tb: cat(only_a(ta), only_b(tb)), ma_tops, mb_tops)
    t_invs = each(lambda nil: jnp.where(eye, 1.0, 0.0) + nil, powers)
    for _ in range(c.bit_length() - 2):
        powers = each(lambda pw: _pdot(pw, pw, ps["inv"]), powers)
        t_invs = each(lambda ti, pw: ti + _pdot(ti, pw, ps["inv"]), t_invs, powers)

    a_news = each(lambda ti, ah: fold(_pdot(ti, cat(only_a(ah), only_b(ah)), ps["apply"])), t_invs, a_hats)
    w1s = each(lambda ta, tb, v: jnp.where(same_head, _pdot(cat(only_b(ta), only_a(tb)), cat(v, v), ps["apply"]), 0.0),
               ma_tops, mb_tops, vs)
    v_news = each(lambda ti, w1: fold(_pdot(ti, w1, ps["apply"])), t_invs, w1s)
    r_news = each(lambda rh, ba, bb, an: rh + fold(_pdot(cat(only_a(ba), only_b(bb)), cat(an, an), ps["apply"])),
                  r_hats, ma_bots, mb_bots, a_news)
    y_ins = each(lambda ba, bb, vn, v: jnp.where(lane_a, _pdot(ba, cat(vn, v), ps["apply"]),
                                                 _pdot(bb, cat(v, vn), ps["apply"])),
                 ma_bots, mb_bots, v_news, vs)
    g_mats = each(lambda be, an, ce: (jnp.where(same_head, _pdot(be.T, an, ps["gh"]), 0.0)
                                      + jnp.where(eye, jnp.broadcast_to(jnp.exp(ce), (n, n)), 0.0)),
                  b_ends, a_news, cum_ends)
    h_mats = each(lambda be, ke, vn, v: jnp.where(same_head, _pdot(cat(be, ke).T, cat(vn, v), ps["gh"]), 0.0),
                  b_ends, k_ends, v_news, vs)
    ys = each(lambda rn, z, yi: _pdot(rn, z, ps["state"]) + yi, r_news, zs, y_ins)
    z_news = each(lambda g, z, h: _pdot(g, z, ps["state"]) + h, g_mats, zs, h_mats)
    return ys, z_news


def _rw_scan_kernel(r_ref, cum_ref, k_ref, v_ref, a_ref, b_ref, g_ref, rk_ref, lnw_ref, lnb_ref,
                    o_ref, z_ref, *, chunk, pairs, passes):
    n = 2 * RWKV_HEAD_DIM
    assert n == LANES and 2 * chunk == n

    @pl.when(pl.program_id(2) == 0)
    def _():
        z_ref[...] = jnp.zeros_like(z_ref)

    sq_r = lax.broadcasted_iota(jnp.int32, (n, n), 0)
    sq_c = lax.broadcasted_iota(jnp.int32, (n, n), 1)
    same_head = (sq_r < RWKV_HEAD_DIM) == (sq_c < RWKV_HEAD_DIM)
    head_mean = jnp.where(same_head, 1.0 / RWKV_HEAD_DIM, 0.0).astype(F32)
    head_ones = jnp.where(same_head, 1.0, 0.0).astype(F32)
    sls = [slice(p * n, (p + 1) * n) for p in range(pairs)]
    take = lambda ref: [ref[:, sl] for sl in sls]
    rs, ks, vs = take(r_ref), take(k_ref), take(v_ref)
    ys, z_news = _rw_chunk(rs, take(cum_ref), ks, vs, take(a_ref), take(b_ref),
                           [z_ref[p] for p in range(pairs)], chunk, passes)
    for p in range(pairs):
        z_ref[p] = z_news[p]
    means = [_pdot(y, head_mean, passes["norm"]) for y in ys]
    devs = [y - m for y, m in zip(ys, means)]
    variances = [_pdot(d * d, head_mean, passes["norm"]) for d in devs]
    bonuses = [_pdot(r * k * rk_ref[:, sl], head_ones, passes["norm"]) * v for r, k, v, sl in zip(rs, ks, vs, sls)]
    for d, var, bonus, sl in zip(devs, variances, bonuses, sls):
        yn = d * lax.rsqrt(var + LNX_EPS) * lnw_ref[:, sl] + lnb_ref[:, sl]
        o_ref[:, sl] = ((yn + bonus) * g_ref[:, sl]).astype(o_ref.dtype)


def rwkv_scan(r, cum, k, v, a_til, b, g, r_k, lnx_w, lnx_b, batch, seq, pairs_per_step=8):
    t = r.shape[0]
    c = RWKV_CHUNK
    nc = seq // c
    n = 2 * RWKV_HEAD_DIM
    pp = pairs_per_step
    groups = RWKV_W // (n * pp)
    tok = pl.BlockSpec((c, n * pp), lambda bi, p, ci: (bi * nc + ci, p))
    par = pl.BlockSpec((1, n * pp), lambda bi, p, ci: (0, p))
    return pl.pallas_call(
        functools.partial(_rw_scan_kernel, chunk=c, pairs=pp, passes=dict(RW_PASSES)),
        grid=(batch, groups, nc),
        in_specs=[tok] * 7 + [par] * 3,
        out_specs=tok,
        out_shape=jax.ShapeDtypeStruct((t, RWKV_W), BF16),
        scratch_shapes=[pltpu.VMEM((pp, n, n), F32)],
        compiler_params=_params("parallel", "parallel", "arbitrary"),
        name="rwkv_scan",
    )(r, cum, k, v, a_til, b, g, r_k.reshape(1, RWKV_W), lnx_w.reshape(1, RWKV_W), lnx_b.reshape(1, RWKV_W))


def _merge_kernel(att_ref, conv_ref, rw_ref, gd_ref, wa_ref, wc_ref, wr_ref,
                  ga_ref, gc_ref, gr_ref, ba_ref, bc_ref, br_ref, o_ref):
    gd = gd_ref[...]

    def branch(x_ref, w_ref, gw_ref, gb_ref):
        y = jnp.dot(x_ref[...], w_ref[0], preferred_element_type=F32)
        gate = jax.nn.sigmoid(jnp.dot(gd, gw_ref[0], preferred_element_type=F32) + gb_ref[0])
        return gate * y

    out = branch(att_ref, wa_ref, ga_ref, ba_ref)
    out = out + branch(conv_ref, wc_ref, gc_ref, bc_ref)
    out = out + branch(rw_ref, wr_ref, gr_ref, br_ref)
    o_ref[...] = out.astype(o_ref.dtype)


def gated_merge(att, conv, rw, gd, w_att_o, w_conv_o, w_rwkv_o, w_gate_up, b_gate, layer):
    t = att.shape[0]
    d = w_att_o.shape[2]
    tm = _tile(t, 1024, SUBLANES)
    tn = _tile(d, 512, LANES)
    nj = d // tn
    act = lambda w: pl.BlockSpec((tm, w), lambda i, j: (i, 0))
    wout = lambda w: pl.BlockSpec((1, w, tn), lambda i, j: (layer, 0, j))
    gate_w = lambda br: pl.BlockSpec((1, MERGE_RANK, tn), lambda i, j: (layer, 0, br * nj + j))
    gate_b = lambda br: pl.BlockSpec((1, 1, tn), lambda i, j: (layer, 0, br * nj + j))
    bg = b_gate.reshape(b_gate.shape[0], 1, N_BRANCH * d)
    return pl.pallas_call(
        _merge_kernel,
        grid=(t // tm, nj),
        in_specs=[act(ATT_W), act(CONV_W), act(RWKV_W), act(MERGE_RANK),
                  wout(ATT_W), wout(CONV_W), wout(RWKV_W),
                  gate_w(0), gate_w(1), gate_w(2), gate_b(0), gate_b(1), gate_b(2)],
        out_specs=pl.BlockSpec((tm, tn), lambda i, j: (i, j)),
        out_shape=jax.ShapeDtypeStruct((t, d), BF16),
        compiler_params=_params("parallel", "parallel"),
        name="gated_merge",
    )(att, conv, rw, gd, w_att_o, w_conv_o, w_rwkv_o, w_gate_up, w_gate_up, w_gate_up, bg, bg, bg)


def _pad_rows(w, rows_before, total):
    return jnp.pad(w, ((rows_before, total - rows_before - w.shape[0]), (0, 0)))


def kernel(x, norm_mix, w_in, w_att_o, conv_w, w_conv_o, rwkv_mu, rwkv_w0, rwkv_w_decay_up, rwkv_a0, rwkv_w_a_up, rwkv_w_g_up, rwkv_k_k, rwkv_k_a, rwkv_r_k, rwkv_lnx_w, rwkv_lnx_b, w_rwkv_o, w_gate_up, b_gate, w_out, norm_mlp, w_mlp_up, w_mlp_down, norm_final):
    batch, seq, d = x.shape
    depth = w_in.shape[0]
    t = batch * seq
    xs = x.reshape(t, d)
    att_end = 3 * ATT_W
    conv_end = att_end + 3 * CONV_W
    rw_end = conv_end + 3 * RWKV_W + LORA_BLOCK + GATE_LORA
    gate_pad = GATE_LORA_PAD - GATE_LORA
    w_in_b = w_in[:, :, :conv_end].astype(BF16)
    w_rw_b = jnp.pad(w_in[:, :, conv_end:rw_end].astype(BF16), ((0, 0), (0, 0), (0, gate_pad)))
    w_gd_b = w_in[:, :, rw_end:].astype(BF16)
    w_att_o_b, w_conv_o_b, w_rwkv_o_b = w_att_o.astype(BF16), w_conv_o.astype(BF16), w_rwkv_o.astype(BF16)
    w_gate_up_b, w_out_b = w_gate_up.astype(BF16), w_out.astype(BF16)
    w_mlp_up_b, w_mlp_down_b = w_mlp_up.astype(BF16), w_mlp_down.astype(BF16)
    q_scale = jnp.concatenate([jnp.full((1, ATT_W), ATT_HEAD_DIM ** -0.5 * LOG2_E, F32), jnp.ones((1, 2 * ATT_W), F32)], axis=1)
    for l in range(depth):
        mu = jnp.pad(rwkv_mu[l], (0, gate_pad))
        wd = _pad_rows(rwkv_w_decay_up[l], 0, LORA_BLOCK)
        wa = _pad_rows(rwkv_w_a_up[l], DECAY_LORA, LORA_BLOCK)
        wg = _pad_rows(rwkv_w_g_up[l], 0, GATE_LORA_PAD)

        h = rmsnorm(xs, norm_mix[l], BF16)
        p_att = matmul(h, w_in_b, l, BF16, epilogue="colscale", extra=q_scale, ncols=att_end)
        p_conv = matmul(h, w_in_b, l, F32, col0=att_end, ncols=conv_end - att_end)
        p_rw = matmul(h, w_rw_b, l, F32, tn=1152)
        gd = matmul(h, w_gd_b, l, BF16)

        y_att = stick_breaking_attention(p_att, batch, seq)
        y_conv = short_gated_conv(p_conv, conv_w[l], seq)
        r, cum, k, v, a_til, b, g = rwkv_prep(p_rw, mu, rwkv_w0[l], wd, rwkv_a0[l], wa, wg,
                                             rwkv_k_k[l], rwkv_k_a[l], seq)
        y_rw = rwkv_scan(r, cum, k, v, a_til, b, g, rwkv_r_k[l], rwkv_lnx_w[l], rwkv_lnx_b[l], batch, seq)

        merged = gated_merge(y_att, y_conv, y_rw, gd, w_att_o_b, w_conv_o_b, w_rwkv_o_b, w_gate_up_b, b_gate, l)
        xs = matmul(merged, w_out_b, l, F32, epilogue="residual", extra=xs)

        h = rmsnorm(xs, norm_mlp[l], BF16)
        up = matmul(h, w_mlp_up_b, l, BF16, epilogue="relu2")
        xs = matmul(up, w_mlp_down_b, l, F32, epilogue="residual", extra=xs, tk=2048)
    return rmsnorm(xs, norm_final, F32).reshape(batch, seq, d)
```
